```python
import jax, jax.numpy as jnp
from jax import lax
import numpy as np

D_MODEL = 1024
BATCH = 8
SEQ = 4096
DEPTH = 4

N_HEADS = 8
HEAD_DIM = 64
V_DIM = 2 * HEAD_DIM
D_ATTN = N_HEADS * V_DIM
D_QK = N_HEADS * 2 * HEAD_DIM
D_CONV = D_MODEL
CONV_WIDTH = 3
D_FF = 2816
ROPE_THETA = 10000.0
Q_BLOCK = 128
EPS = 1e-6
N_MOD = 9
D_IN = 3 * D_CONV + 2 * D_QK + D_ATTN + 2 * D_MODEL

kernel_name = "hybrid_shortconv_diffattn_macaron_adaln"


def rms_norm(x, g):
    xf = x.astype(jnp.float32)
    xf = xf * lax.rsqrt(jnp.mean(xf * xf, axis=-1, keepdims=True) + EPS)
    return xf.astype(x.dtype) * g


def rope_tables(positions, dtype):
    inv_freq = 1.0 / (ROPE_THETA ** (jnp.arange(0, HEAD_DIM // 2, dtype=jnp.float32) * (2.0 / HEAD_DIM)))
    ang = positions.astype(jnp.float32)[..., None] * inv_freq
    return (jnp.cos(ang)[:, :, None, None, :].astype(dtype),
            jnp.sin(ang)[:, :, None, None, :].astype(dtype))


def apply_rope(t, cos, sin):
    t1, t2 = jnp.split(t, 2, axis=-1)
    return jnp.concatenate([t1 * cos - t2 * sin, t2 * cos + t1 * sin], axis=-1)


def swiglu(h, w_gu, w_down):
    g, u = jnp.split(h @ w_gu, 2, axis=-1)
    return (jax.nn.silu(g) * u) @ w_down


def causal_depthwise_conv(u, w):
    return lax.conv_general_dilated(
        u, w[:, None, :].astype(u.dtype), window_strides=(1,),
        padding=((CONV_WIDTH - 1, 0),), dimension_numbers=("NWC", "WIO", "NWC"),
        feature_group_count=u.shape[-1])


def diff_attention(q, k, v, lam):
    qh = q.transpose(0, 2, 3, 1, 4)
    kh = k.transpose(0, 2, 3, 1, 4)
    vh = v.transpose(0, 2, 1, 3)
    seq = q.shape[1]
    outs = []
    for start in range(0, seq, Q_BLOCK):
        end = start + Q_BLOCK
        s = jnp.einsum('bhcqd,bhckd->bhcqk', qh[:, :, :, start:end], kh[:, :, :, :end]).astype(jnp.float32)
        causal = jnp.arange(end)[None, :] <= jnp.arange(start, end)[:, None]
        p = jax.nn.softmax(jnp.where(causal, s, -jnp.inf), axis=-1)
        a = p[:, :, 0] - lam * p[:, :, 1]
        outs.append(jnp.einsum('bhqk,bhkd->bhqd', a.astype(vh.dtype), vh[:, :, :end]))
    return jnp.concatenate(outs, axis=2)


def split_combined(proj):
    sizes = (D_CONV, D_CONV, D_CONV, D_QK, D_QK, D_ATTN, D_MODEL, D_MODEL)
    idx = [int(i) for i in np.cumsum(sizes)[:-1]]
    return jnp.split(proj, idx, axis=-1)


def setup_inputs(seed: int = 0) -> dict:
    key = jax.random.key(seed)
    ks = jax.random.split(key, 24)
    f32 = jnp.float32

    def nrm(k, shape, fan_in, gain=1.0):
        return jax.random.normal(k, shape, f32) * (gain * fan_in ** -0.5)

    x = jax.random.normal(ks[0], (BATCH, SEQ, D_MODEL), f32)
    c = jax.random.normal(ks[1], (BATCH, D_MODEL), f32)
    offs = jax.random.randint(ks[2], (BATCH, 1), 0, 1024, dtype=jnp.int32)
    positions = offs + jnp.arange(SEQ, dtype=jnp.int32)[None, :]
    return {
        "x": x,
        "c": c,
        "positions": positions,
        "norm_g": 1.0 + 0.02 * jax.random.normal(ks[3], (DEPTH, 3, D_MODEL), f32),
        "w_ada": nrm(ks[4], (DEPTH, D_MODEL, N_MOD * D_MODEL), D_MODEL, 0.5),
        "b_ada": 0.02 * jax.random.normal(ks[5], (DEPTH, N_MOD * D_MODEL), f32),
        "w_ffn1_gu": nrm(ks[6], (DEPTH, D_MODEL, 2 * D_FF), D_MODEL),
        "w_ffn1_down": nrm(ks[7], (DEPTH, D_FF, D_MODEL), D_FF),
        "w_in": nrm(ks[8], (DEPTH, D_MODEL, D_IN), D_MODEL),
        "conv_w": nrm(ks[9], (DEPTH, CONV_WIDTH, D_CONV), CONV_WIDTH),
        "q_norm_g": 1.0 + 0.02 * jax.random.normal(ks[10], (DEPTH, HEAD_DIM), f32),
        "k_norm_g": 1.0 + 0.02 * jax.random.normal(ks[11], (DEPTH, HEAD_DIM), f32),
        "lambda_q1": 0.1 * jax.random.normal(ks[12], (DEPTH, HEAD_DIM), f32),
        "lambda_k1": 0.1 * jax.random.normal(ks[13], (DEPTH, HEAD_DIM), f32),
        "lambda_q2": 0.1 * jax.random.normal(ks[14], (DEPTH, HEAD_DIM), f32),
        "lambda_k2": 0.1 * jax.random.normal(ks[15], (DEPTH, HEAD_DIM), f32),
        "subln_g": 1.0 + 0.02 * jax.random.normal(ks[16], (DEPTH, V_DIM), f32),
        "w_conv_out": nrm(ks[17], (DEPTH, D_CONV, D_MODEL), D_CONV),
        "w_attn_out": nrm(ks[18], (DEPTH, D_ATTN, D_MODEL), D_ATTN),
        "w_o": nrm(ks[19], (DEPTH, D_MODEL, D_MODEL), D_MODEL),
        "w_ffn2_gu": nrm(ks[20], (DEPTH, D_MODEL, 2 * D_FF), D_MODEL),
        "w_ffn2_down": nrm(ks[21], (DEPTH, D_FF, D_MODEL), D_FF),
    }


def reference(x, c, positions, norm_g, w_ada, b_ada, w_ffn1_gu, w_ffn1_down, w_in, conv_w,
              q_norm_g, k_norm_g, lambda_q1, lambda_k1, lambda_q2, lambda_k2, subln_g,
              w_conv_out, w_attn_out, w_o, w_ffn2_gu, w_ffn2_down):
    bsz, seq, _ = x.shape
    cos, sin = rope_tables(positions, x.dtype)
    c_act = jax.nn.silu(c)
    q_scale = HEAD_DIM ** -0.5
    for l in range(DEPTH):
        lam_init = 0.8 - 0.6 * float(np.exp(-0.3 * l))
        mod = (c_act @ w_ada[l] + b_ada[l])[:, None, :]
        (sh1, sc1, g1, sh2, sc2, g2, sh3, sc3, g3) = jnp.split(mod, N_MOD, axis=-1)

        h = rms_norm(x, norm_g[l, 0]) * (1.0 + sc1) + sh1
        x = x + 0.5 * g1 * swiglu(h, w_ffn1_gu[l], w_ffn1_down[l])

        h = rms_norm(x, norm_g[l, 1]) * (1.0 + sc2) + sh2
        b_c, c_c, x_c, q, k, v, gate_conv, gate_attn = split_combined(h @ w_in[l])

        y_conv = b_c * causal_depthwise_conv(c_c * x_c, conv_w[l])
        y_conv = y_conv @ w_conv_out[l]

        q = q.reshape(bsz, seq, N_HEADS, 2, HEAD_DIM)
        k = k.reshape(bsz, seq, N_HEADS, 2, HEAD_DIM)
        v = v.reshape(bsz, seq, N_HEADS, V_DIM)
        q = apply_rope(rms_norm(q, q_norm_g[l]), cos, sin) * q_scale
        k = apply_rope(rms_norm(k, k_norm_g[l]), cos, sin)
        lam = (jnp.exp(jnp.sum(lambda_q1[l].astype(jnp.float32) * lambda_k1[l].astype(jnp.float32)))
               - jnp.exp(jnp.sum(lambda_q2[l].astype(jnp.float32) * lambda_k2[l].astype(jnp.float32)))
               + lam_init)
        o = diff_attention(q, k, v, lam)
        o = rms_norm(o, subln_g[l]) * (1.0 - lam_init)
        o = o.transpose(0, 2, 1, 3).reshape(bsz, seq, D_ATTN)
        y_attn = o @ w_attn_out[l]

        merged = jax.nn.sigmoid(gate_conv) * y_conv + jax.nn.sigmoid(gate_attn) * y_attn
        x = x + g2 * (merged @ w_o[l])

        h = rms_norm(x, norm_g[l, 2]) * (1.0 + sc3) + sh3
        x = x + 0.5 * g3 * swiglu(h, w_ffn2_gu[l], w_ffn2_down[l])
    return x
```

```python
import functools

import jax
import jax.numpy as jnp
import numpy as np
from jax import lax
from jax.experimental import pallas as pl
from jax.experimental.pallas import tpu as pltpu

N_HEADS = 8
HEAD_DIM = 64
V_DIM = 2 * HEAD_DIM
CONV_WIDTH = 3
ROPE_THETA = 10000.0
EPS = 1e-6
N_MOD = 9

V7X_LANES = 128
V7X_SUBLANES = 8
V7X_MXU_DIM = 256
VMEM_LIMIT_BYTES = 56 * 1024 * 1024

TOKEN_TILE = 512
COL_CHUNK = V7X_MXU_DIM
ATTN_TILE = 512

BF16 = jnp.bfloat16
F32 = jnp.float32


def _sigmoid(x):
    return 1.0 / (1.0 + jnp.exp(-x))


def _dot(a, b):
    return jnp.dot(a, b, preferred_element_type=F32)


def _resident(shape):
    return pl.BlockSpec(shape, lambda *_: (0,) * len(shape), pipeline_mode=pl.Buffered(1))


def _params():
    return pltpu.CompilerParams(
        dimension_semantics=("arbitrary", "arbitrary"), vmem_limit_bytes=VMEM_LIMIT_BYTES)


def _modulated_norm(x, g_row, scale, shift):
    ms = jnp.mean(x * x, axis=-1, keepdims=True)
    return (x * lax.rsqrt(ms + EPS)) * g_row * (1.0 + scale) + shift


def _ada_kernel(c_ref, w_ref, b_ref, o_ref):
    c = c_ref[...]
    c_act = (c * _sigmoid(c)).astype(BF16)
    o_ref[0] = _dot(c_act, w_ref[0].astype(BF16)) + b_ref[0]


def _ada_call(c, w_ada, b_ada):
    depth, d, n = w_ada.shape
    bsz = c.shape[0]
    tn = d
    return pl.pallas_call(
        _ada_kernel,
        grid=(depth, n // tn),
        in_specs=[
            pl.BlockSpec((bsz, d), lambda l, j: (0, 0)),
            pl.BlockSpec((1, d, tn), lambda l, j: (l, 0, j)),
            pl.BlockSpec((1, 1, tn), lambda l, j: (l, 0, j)),
        ],
        out_specs=pl.BlockSpec((1, bsz, tn), lambda l, j: (l, 0, j)),
        out_shape=jax.ShapeDtypeStruct((depth, bsz, n), F32),
        compiler_params=_params(),
        name="ada_mod",
    )(c, w_ada, b_ada.reshape(depth, 1, n))


def _ffn_kernel(x_ref, mod_ref, ng_ref, wgu_ref, wd_ref, o_ref, h_ref, a_ref, *, mod_base):
    d_ff = wd_ref.shape[0]
    shift = mod_ref[0, mod_base:mod_base + 1, :]
    scale = mod_ref[0, mod_base + 1:mod_base + 2, :]
    gate = mod_ref[0, mod_base + 2:mod_base + 3, :]
    h_ref[...] = _modulated_norm(x_ref[0], ng_ref[...], scale, shift).astype(BF16)
    for lo in range(0, d_ff, COL_CHUNK):
        h = h_ref[...]
        g = _dot(h, wgu_ref[:, lo:lo + COL_CHUNK])
        u = _dot(h, wgu_ref[:, d_ff + lo:d_ff + lo + COL_CHUNK])
        a_ref[:, lo:lo + COL_CHUNK] = (g * _sigmoid(g) * u).astype(BF16)
    y = _dot(a_ref[...], wd_ref[...])
    o_ref[0] = x_ref[0] + 0.5 * gate * y


def _ffn_call(x, mod, ng, wgu, wd, mod_base):
    bsz, seq, d = x.shape
    d_ff = wd.shape[0]
    tm = TOKEN_TILE
    tile = pl.BlockSpec((1, tm, d), lambda b, i: (b, i, 0))
    return pl.pallas_call(
        functools.partial(_ffn_kernel, mod_base=mod_base),
        grid=(bsz, seq // tm),
        in_specs=[
            tile,
            pl.BlockSpec((1, N_MOD, d), lambda b, i: (b, 0, 0)),
            _resident((1, d)),
            _resident((d, 2 * d_ff)),
            _resident((d_ff, d)),
        ],
        out_specs=tile,
        out_shape=jax.ShapeDtypeStruct(x.shape, F32),
        scratch_shapes=[pltpu.VMEM((tm, d), BF16), pltpu.VMEM((tm, d_ff), BF16)],
        compiler_params=_params(),
        name="ffn",
    )(x, mod, ng, wgu, wd)


def _proj_kernel(x_ref, mod_ref, ng_ref, win_ref, cw_ref, wco_ref, rope_ref, qkg_ref, gmat_ref,
                 q_ref, k_ref, v_ref, yc_ref, ga_ref, h_ref, upad_ref, ypre_ref):
    tm, d = h_ref.shape
    halo = V7X_SUBLANES

    @pl.when(pl.program_id(1) == 0)
    def _():
        upad_ref[0:halo, :] = jnp.zeros((halo, d), F32)

    shift = mod_ref[0, 3:4, :]
    scale = mod_ref[0, 4:5, :]
    h_ref[...] = _modulated_norm(x_ref[0], ng_ref[...], scale, shift).astype(BF16)

    def proj(col):
        return _dot(h_ref[...], win_ref[:, col:col + COL_CHUNK])

    for lo in range(0, d, COL_CHUNK):
        cols = slice(lo, lo + COL_CHUNK)
        b_c = proj(lo)
        u = proj(d + lo) * proj(2 * d + lo)
        upad_ref[halo:halo + tm, cols] = u
        conv = (cw_ref[2:3, cols] * u
                + cw_ref[1:2, cols] * upad_ref[halo - 1:halo - 1 + tm, cols]
                + cw_ref[0:1, cols] * upad_ref[halo - 2:halo - 2 + tm, cols])
        upad_ref[0:halo, cols] = upad_ref[tm:tm + halo, cols]
        ypre_ref[:, cols] = (b_c * conv).astype(BF16)
    for lo in range(0, d, COL_CHUNK):
        cols = slice(lo, lo + COL_CHUNK)
        y_conv = _dot(ypre_ref[...], wco_ref[:, cols])
        yc_ref[0, :, cols] = (_sigmoid(proj(6 * d + lo)) * y_conv).astype(BF16)

    cos_t = rope_ref[0, :, 0:V7X_LANES]
    sin_lo = rope_ref[0, :, V7X_LANES:2 * V7X_LANES]
    sin_hi = rope_ref[0, :, 2 * V7X_LANES:3 * V7X_LANES]
    half = HEAD_DIM // 2
    for base, out_ref, row in ((3 * d, q_ref, 0), (4 * d, k_ref, 2)):
        c_tab = cos_t * qkg_ref[row:row + 1, :]
        lo_tab = sin_lo * qkg_ref[row + 1:row + 2, :]
        hi_tab = sin_hi * qkg_ref[row + 1:row + 2, :]
        for lo in range(0, d, COL_CHUNK):
            t = proj(base + lo)
            ms = _dot((t * t).astype(BF16), gmat_ref[...])
            tn = t * lax.rsqrt(ms + EPS)
            for off in range(0, COL_CHUNK, V7X_LANES):
                th = tn[:, off:off + V7X_LANES]
                r = (th * c_tab
                     + pltpu.roll(th, V7X_LANES - half, axis=1) * lo_tab
                     + pltpu.roll(th, half, axis=1) * hi_tab)
                out_ref[0, :, lo + off:lo + off + V7X_LANES] = r.astype(BF16)

    for lo in range(0, d, COL_CHUNK):
        cols = slice(lo, lo + COL_CHUNK)
        v_ref[0, :, cols] = proj(5 * d + lo).astype(BF16)
        ga_ref[0, :, cols] = _sigmoid(proj(7 * d + lo)).astype(BF16)


def _proj_call(x, mod, ng, win, cw, wco, rope, qkg, gmat):
    bsz, seq, d = x.shape
    tm = TOKEN_TILE
    tile = pl.BlockSpec((1, tm, d), lambda b, i: (b, i, 0))
    act = jax.ShapeDtypeStruct(x.shape, BF16)
    return pl.pallas_call(
        _proj_kernel,
        grid=(bsz, seq // tm),
        in_specs=[
            tile,
            pl.BlockSpec((1, N_MOD, d), lambda b, i: (b, 0, 0)),
            _resident((1, d)),
            _resident(win.shape),
            _resident(cw.shape),
            _resident(wco.shape),
            pl.BlockSpec((1, tm, 3 * V7X_LANES), lambda b, i: (b, i, 0)),
            _resident(qkg.shape),
            _resident(gmat.shape),
        ],
        out_specs=[tile] * 5,
        out_shape=[act] * 5,
        scratch_shapes=[
            pltpu.VMEM((tm, d), BF16),
            pltpu.VMEM((tm + V7X_SUBLANES, d), F32),
            pltpu.VMEM((tm, d), BF16),
        ],
        compiler_params=_params(),
        name="proj",
    )(x, mod, ng, win, cw, wco, rope, qkg, gmat)


def _attn_kernel(q_ref, k_ref, v_ref, lq1_ref, lk1_ref, lq2_ref, lk2_ref, sg_ref, o_ref,
                 qc_ref, m_ref, l_ref, acc_ref, *, lam_init):
    t = q_ref.shape[1]
    qi = pl.program_id(2)
    lane = lax.broadcasted_iota(jnp.int32, (t, V_DIM), 1)
    q = q_ref[0]
    zero = jnp.zeros_like(q)
    qc_ref[0] = jnp.where(lane < HEAD_DIM, q, zero)
    qc_ref[1] = jnp.where(lane >= HEAD_DIM, q, zero)
    m_ref[...] = jnp.full(m_ref.shape, -jnp.inf, F32)
    l_ref[...] = jnp.zeros(l_ref.shape, F32)
    acc_ref[...] = jnp.zeros(acc_ref.shape, F32)

    def step(j, masked):
        start = pl.multiple_of(j * t, t)
        kblk = k_ref[0, pl.ds(start, t), :]
        vblk = v_ref[0, pl.ds(start, t), :]
        for c in range(2):
            s = lax.dot_general(qc_ref[c], kblk, (((1,), (1,)), ((), ())),
                                preferred_element_type=F32)
            if masked:
                row = lax.broadcasted_iota(jnp.int32, (t, t), 0)
                col = lax.broadcasted_iota(jnp.int32, (t, t), 1)
                s = jnp.where(row >= col, s, -jnp.inf)
            m_prev = m_ref[c]
            m_new = jnp.maximum(m_prev, jnp.max(s, axis=1, keepdims=True))
            alpha = jnp.exp(m_prev - m_new)
            p = jnp.exp(s - m_new)
            l_ref[c] = alpha * l_ref[c] + jnp.sum(p, axis=1, keepdims=True)
            acc_ref[c] = alpha * acc_ref[c] + _dot(p.astype(BF16), vblk)
            m_ref[c] = m_new

    def body(j, carry):
        step(j, False)
        return carry

    lax.fori_loop(0, qi, body, 0)
    step(qi, True)

    lam = (jnp.exp(jnp.sum(lq1_ref[...] * lk1_ref[...], axis=-1, keepdims=True))
           - jnp.exp(jnp.sum(lq2_ref[...] * lk2_ref[...], axis=-1, keepdims=True))
           + lam_init)
    o = acc_ref[0] / l_ref[0] - lam * (acc_ref[1] / l_ref[1])
    ms = jnp.mean(o * o, axis=-1, keepdims=True)
    o = (o * lax.rsqrt(ms + EPS)) * sg_ref[...] * (1.0 - lam_init)
    o_ref[0] = o.astype(BF16)


def _attn_call(q, k, v, lq1, lk1, lq2, lk2, sg, lam_init):
    bsz, seq, d = q.shape
    t = ATTN_TILE
    vec = pl.BlockSpec((1, HEAD_DIM), lambda b, h, i: (0, 0))
    kv = pl.BlockSpec((1, seq, V_DIM), lambda b, h, i: (b, 0, h))
    qo = pl.BlockSpec((1, t, V_DIM), lambda b, h, i: (b, i, h))
    return pl.pallas_call(
        functools.partial(_attn_kernel, lam_init=lam_init),
        grid=(bsz, N_HEADS, seq // t),
        in_specs=[qo, kv, kv, vec, vec, vec, vec,
                  pl.BlockSpec((1, V_DIM), lambda b, h, i: (0, 0))],
        out_specs=qo,
        out_shape=jax.ShapeDtypeStruct(q.shape, BF16),
        scratch_shapes=[
            pltpu.VMEM((2, t, V_DIM), BF16),
            pltpu.VMEM((2, t, 1), F32),
            pltpu.VMEM((2, t, 1), F32),
            pltpu.VMEM((2, t, V_DIM), F32),
        ],
        compiler_params=pltpu.CompilerParams(
            dimension_semantics=("arbitrary",) * 3, vmem_limit_bytes=VMEM_LIMIT_BYTES),
        name="diff_attn",
    )(q, k, v, lq1, lk1, lq2, lk2, sg)


def _merge_kernel(x_ref, mod_ref, o_ref, yc_ref, ga_ref, wao_ref, wo_ref, out_ref):
    y_attn = _dot(o_ref[0], wao_ref[...])
    merged = yc_ref[0].astype(F32) + ga_ref[0].astype(F32) * y_attn
    y = _dot(merged.astype(BF16), wo_ref[...])
    out_ref[0] = x_ref[0] + mod_ref[0, 5:6, :] * y


def _merge_call(x, mod, o, yc, ga, wao, wo):
    bsz, seq, d = x.shape
    tm = TOKEN_TILE
    tile = pl.BlockSpec((1, tm, d), lambda b, i: (b, i, 0))
    return pl.pallas_call(
        _merge_kernel,
        grid=(bsz, seq // tm),
        in_specs=[tile, pl.BlockSpec((1, N_MOD, d), lambda b, i: (b, 0, 0)),
                  tile, tile, tile, _resident(wao.shape), _resident(wo.shape)],
        out_specs=tile,
        out_shape=jax.ShapeDtypeStruct(x.shape, F32),
        compiler_params=_params(),
        name="merge",
    )(x, mod, o, yc, ga, wao, wo)


def _rope_tables(positions):
    half = HEAD_DIM // 2
    inv_freq = 1.0 / (ROPE_THETA ** (jnp.arange(0, half, dtype=F32) * (2.0 / HEAD_DIM)))
    ang = positions.astype(F32)[..., None] * inv_freq
    cos, sin = jnp.cos(ang), jnp.sin(ang)
    zero = jnp.zeros_like(sin)
    reps = V7X_LANES // HEAD_DIM
    return jnp.concatenate([cos] * (2 * reps) + [-sin, zero] * reps + [zero, sin] * reps, axis=-1)


def _group_mean_matrix():
    g = np.arange(V7X_MXU_DIM) // HEAD_DIM
    return jnp.asarray((g[:, None] == g[None, :]).astype(np.float32) / HEAD_DIM, dtype=BF16)


def kernel(x, c, positions, norm_g, w_ada, b_ada, w_ffn1_gu, w_ffn1_down, w_in, conv_w, q_norm_g, k_norm_g, lambda_q1, lambda_k1, lambda_q2, lambda_k2, subln_g, w_conv_out, w_attn_out, w_o, w_ffn2_gu, w_ffn2_down):
    bsz, seq, d = x.shape
    depth = norm_g.shape[0]
    assert seq % TOKEN_TILE == 0 and seq % ATTN_TILE == 0 and d % COL_CHUNK == 0
    assert w_ffn1_down.shape[1] % COL_CHUNK == 0

    mods = _ada_call(c, w_ada, b_ada).reshape(depth, bsz, N_MOD, d)
    rope = _rope_tables(positions)
    gmat = _group_mean_matrix()
    reps = V7X_LANES // HEAD_DIM
    half = HEAD_DIM // 2

    def lane_rows(g, scale):
        partner = jnp.concatenate([g[half:], g[:half]])
        return [jnp.tile(g, reps) * scale, jnp.tile(partner, reps) * scale]

    for l in range(depth):
        lam_init = 0.8 - 0.6 * float(np.exp(-0.3 * l))
        mod = mods[l]
        ng = norm_g[l]
        qkg = jnp.stack(lane_rows(q_norm_g[l], HEAD_DIM ** -0.5) + lane_rows(k_norm_g[l], 1.0))
        vec = lambda a: a[l].reshape(1, -1)

        x = _ffn_call(x, mod, ng[0:1], w_ffn1_gu[l].astype(BF16), w_ffn1_down[l].astype(BF16), 0)
        q, k, v, yc, ga = _proj_call(x, mod, ng[1:2], w_in[l].astype(BF16), conv_w[l],
                                     w_conv_out[l].astype(BF16), rope, qkg, gmat)
        o = _attn_call(q, k, v, vec(lambda_q1), vec(lambda_k1), vec(lambda_q2), vec(lambda_k2),
                       vec(subln_g), lam_init)
        x = _merge_call(x, mod, o, yc, ga, w_attn_out[l].astype(BF16), w_o[l].astype(BF16))
        x = _ffn_call(x, mod, ng[2:3], w_ffn2_gu[l].astype(BF16), w_ffn2_down[l].astype(BF16), 6)
    return x
```

```python
import functools

import jax
import jax.numpy as jnp
import numpy as np
from jax import lax
from jax.experimental import pallas as pl
from jax.experimental.pallas import tpu as pltpu

N_HEADS = 8
HEAD_DIM = 64
V_DIM = 2 * HEAD_DIM
CONV_WIDTH = 3
ROPE_THETA = 10000.0
EPS = 1e-6
N_MOD = 9

V7X_LANES = 128
V7X_SUBLANES = 8
V7X_MXU_DIM = 256
VMEM_LIMIT_BYTES = 56 * 1024 * 1024

TOKEN_TILE = 512
COL_CHUNK = V7X_MXU_DIM
ATTN_KEY_TILE = 512
ATTN_QUERY_TILE = 2 * ATTN_KEY_TILE
MAX_FIXED_SHIFT = 48.0
LOG2_E = float(np.log2(np.e))

BF16 = jnp.bfloat16
F32 = jnp.float32


def _sigmoid(x):
    return 1.0 / (1.0 + jnp.exp(-x))


def _dot(a, b):
    return jnp.dot(a, b, preferred_element_type=F32)


def _resident(shape):
    return pl.BlockSpec(shape, lambda *_: (0,) * len(shape), pipeline_mode=pl.Buffered(1))


def _params():
    return pltpu.CompilerParams(
        dimension_semantics=("arbitrary", "arbitrary"), vmem_limit_bytes=VMEM_LIMIT_BYTES)


def _modulated_norm(x, g_row, scale, shift):
    ms = jnp.mean(x * x, axis=-1, keepdims=True)
    return (x * lax.rsqrt(ms + EPS)) * g_row * (1.0 + scale) + shift


def _ada_kernel(c_ref, w_ref, b_ref, o_ref):
    c = c_ref[...]
    c_act = (c * _sigmoid(c)).astype(BF16)
    o_ref[0] = _dot(c_act, w_ref[0].astype(BF16)) + b_ref[0]


def _ada_call(c, w_ada, b_ada):
    depth, d, n = w_ada.shape
    bsz = c.shape[0]
    tn = d
    return pl.pallas_call(
        _ada_kernel,
        grid=(depth, n // tn),
        in_specs=[
            pl.BlockSpec((bsz, d), lambda l, j: (0, 0)),
            pl.BlockSpec((1, d, tn), lambda l, j: (l, 0, j)),
            pl.BlockSpec((1, 1, tn), lambda l, j: (l, 0, j)),
        ],
        out_specs=pl.BlockSpec((1, bsz, tn), lambda l, j: (l, 0, j)),
        out_shape=jax.ShapeDtypeStruct((depth, bsz, n), F32),
        compiler_params=_params(),
        name="ada_mod",
    )(c, w_ada, b_ada.reshape(depth, 1, n))


def _ffn_kernel(x_ref, mod_ref, ng_ref, wgu_ref, wd_ref, o_ref, h_ref, a_ref, *, mod_base):
    d_ff = wd_ref.shape[0]
    shift = mod_ref[0, mod_base:mod_base + 1, :]
    scale = mod_ref[0, mod_base + 1:mod_base + 2, :]
    gate = mod_ref[0, mod_base + 2:mod_base + 3, :]
    h_ref[...] = _modulated_norm(x_ref[0], ng_ref[...], scale, shift).astype(BF16)
    for lo in range(0, d_ff, COL_CHUNK):
        h = h_ref[...]
        g = _dot(h, wgu_ref[:, lo:lo + COL_CHUNK])
        u = _dot(h, wgu_ref[:, d_ff + lo:d_ff + lo + COL_CHUNK])
        a_ref[:, lo:lo + COL_CHUNK] = (g * _sigmoid(g) * u).astype(BF16)
    y = _dot(a_ref[...], wd_ref[...])
    o_ref[0] = x_ref[0] + 0.5 * gate * y


def _ffn_call(x, mod, ng, wgu, wd, mod_base):
    bsz, seq, d = x.shape
    d_ff = wd.shape[0]
    tm = TOKEN_TILE
    tile = pl.BlockSpec((1, tm, d), lambda b, i: (b, i, 0))
    return pl.pallas_call(
        functools.partial(_ffn_kernel, mod_base=mod_base),
        grid=(bsz, seq // tm),
        in_specs=[
            tile,
            pl.BlockSpec((1, N_MOD, d), lambda b, i: (b, 0, 0)),
            _resident((1, d)),
            _resident((d, 2 * d_ff)),
            _resident((d_ff, d)),
        ],
        out_specs=tile,
        out_shape=jax.ShapeDtypeStruct(x.shape, F32),
        scratch_shapes=[pltpu.VMEM((tm, d), BF16), pltpu.VMEM((tm, d_ff), BF16)],
        compiler_params=_params(),
        name="ffn",
    )(x, mod, ng, wgu, wd)


def _proj_kernel(x_ref, mod_ref, ng_ref, win_ref, cw_ref, wco_ref, rope_ref, qkg_ref, gmat_ref,
                 q_ref, k_ref, v_ref, yc_ref, ga_ref, h_ref, upad_ref, ypre_ref):
    tm, d = h_ref.shape
    halo = V7X_SUBLANES

    @pl.when(pl.program_id(1) == 0)
    def _():
        upad_ref[0:halo, :] = jnp.zeros((halo, d), F32)

    shift = mod_ref[0, 3:4, :]
    scale = mod_ref[0, 4:5, :]
    h_ref[...] = _modulated_norm(x_ref[0], ng_ref[...], scale, shift).astype(BF16)

    def proj(col):
        return _dot(h_ref[...], win_ref[:, col:col + COL_CHUNK])

    for lo in range(0, d, COL_CHUNK):
        cols = slice(lo, lo + COL_CHUNK)
        b_c = proj(lo)
        u = proj(d + lo) * proj(2 * d + lo)
        upad_ref[halo:halo + tm, cols] = u
        conv = (cw_ref[2:3, cols] * u
                + cw_ref[1:2, cols] * upad_ref[halo - 1:halo - 1 + tm, cols]
                + cw_ref[0:1, cols] * upad_ref[halo - 2:halo - 2 + tm, cols])
        upad_ref[0:halo, cols] = upad_ref[tm:tm + halo, cols]
        ypre_ref[:, cols] = (b_c * conv).astype(BF16)
    for lo in range(0, d, COL_CHUNK):
        cols = slice(lo, lo + COL_CHUNK)
        y_conv = _dot(ypre_ref[...], wco_ref[:, cols])
        yc_ref[0, :, cols] = (_sigmoid(proj(6 * d + lo)) * y_conv).astype(BF16)

    cos_t = rope_ref[0, :, 0:V7X_LANES]
    sin_lo = rope_ref[0, :, V7X_LANES:2 * V7X_LANES]
    sin_hi = rope_ref[0, :, 2 * V7X_LANES:3 * V7X_LANES]
    half = HEAD_DIM // 2
    for base, out_ref, row in ((3 * d, q_ref, 0), (4 * d, k_ref, 2)):
        c_tab = cos_t * qkg_ref[row:row + 1, :]
        lo_tab = sin_lo * qkg_ref[row + 1:row + 2, :]
        hi_tab = sin_hi * qkg_ref[row + 1:row + 2, :]
        for lo in range(0, d, COL_CHUNK):
            t = proj(base + lo)
            ms = _dot((t * t).astype(BF16), gmat_ref[...])
            tn = t * lax.rsqrt(ms + EPS)
            for off in range(0, COL_CHUNK, V7X_LANES):
                th = tn[:, off:off + V7X_LANES]
                r = (th * c_tab
                     + pltpu.roll(th, V7X_LANES - half, axis=1) * lo_tab
                     + pltpu.roll(th, half, axis=1) * hi_tab)
                out_ref[0, :, lo + off:lo + off + V7X_LANES] = r.astype(BF16)

    for lo in range(0, d, COL_CHUNK):
        cols = slice(lo, lo + COL_CHUNK)
        v_ref[0, :, cols] = proj(5 * d + lo).astype(BF16)
        ga_ref[0, :, cols] = _sigmoid(proj(7 * d + lo)).astype(BF16)


def _proj_call(x, mod, ng, win, cw, wco, rope, qkg, gmat):
    bsz, seq, d = x.shape
    tm = TOKEN_TILE
    tile = pl.BlockSpec((1, tm, d), lambda b, i: (b, i, 0))
    act = jax.ShapeDtypeStruct(x.shape, BF16)
    return pl.pallas_call(
        _proj_kernel,
        grid=(bsz, seq // tm),
        in_specs=[
            tile,
            pl.BlockSpec((1, N_MOD, d), lambda b, i: (b, 0, 0)),
            _resident((1, d)),
            _resident(win.shape),
            _resident(cw.shape),
            _resident(wco.shape),
            pl.BlockSpec((1, tm, 3 * V7X_LANES), lambda b, i: (b, i, 0)),
            _resident(qkg.shape),
            _resident(gmat.shape),
        ],
        out_specs=[tile] * 5,
        out_shape=[act] * 5,
        scratch_shapes=[
            pltpu.VMEM((tm, d), BF16),
            pltpu.VMEM((tm + V7X_SUBLANES, d), F32),
            pltpu.VMEM((tm, d), BF16),
        ],
        compiler_params=_params(),
        name="proj",
    )(x, mod, ng, win, cw, wco, rope, qkg, gmat)


def _attn_kernel(shift_ref, q_ref, k_ref, v_ref, lq1_ref, lk1_ref, lq2_ref, lk2_ref, sg_ref, o_ref,
                 qc_ref, vaug_ref, m_ref, l_ref, acc_ref, *, lam_init):
    tq = q_ref.shape[1]
    tk = ATTN_KEY_TILE
    qi = pl.program_id(2)
    shift = shift_ref[0]

    @pl.when(qi == 0)
    def _():
        vaug_ref[:, 0:V_DIM] = v_ref[0]
        vaug_ref[:, V_DIM:2 * V_DIM] = jnp.ones((vaug_ref.shape[0], V_DIM), BF16)

    lane = lax.broadcasted_iota(jnp.int32, (tq, V_DIM), 1)
    q = q_ref[0]
    zero = jnp.zeros_like(q)
    qc_ref[0] = jnp.where(lane < HEAD_DIM, q, zero)
    qc_ref[1] = jnp.where(lane >= HEAD_DIM, q, zero)
    acc_ref[...] = jnp.zeros(acc_ref.shape, F32)

    def scores(c, rows, start):
        return lax.dot_general(qc_ref[c, rows, :], k_ref[0, pl.ds(start, tk), :],
                               (((1,), (1,)), ((), ())), preferred_element_type=F32)

    def causal(x, fill):
        row = lax.broadcasted_iota(jnp.int32, x.shape, 0)
        col = lax.broadcasted_iota(jnp.int32, x.shape, 1)
        return jnp.where(row >= col, x, fill)

    def fixed_step(j, row0, on_diagonal):
        start = pl.multiple_of(j * tk, tk)
        rows = slice(row0, tq)
        for c in range(2):
            p = jnp.exp2(scores(c, rows, start) - shift)
            if on_diagonal:
                p = causal(p, 0.0)
            acc_ref[c, rows, :] += _dot(p.astype(BF16), vaug_ref[pl.ds(start, tk), :])

    def online_step(j, row0, on_diagonal):
        start = pl.multiple_of(j * tk, tk)
        rows = slice(row0, tq)
        for c in range(2):
            s = scores(c, rows, start)
            if on_diagonal:
                s = causal(s, -jnp.inf)
            m_prev = m_ref[c, rows, :]
            m_new = jnp.maximum(m_prev, jnp.max(s, axis=1, keepdims=True))
            alpha = jnp.exp2(m_prev - m_new)
            p = jnp.exp2(s - m_new)
            l_ref[c, rows, :] = alpha * l_ref[c, rows, :] + jnp.sum(p, axis=1, keepdims=True)
            acc_ref[c, rows, 0:V_DIM] = (alpha * acc_ref[c, rows, 0:V_DIM]
                                         + _dot(p.astype(BF16), v_ref[0, pl.ds(start, tk), :]))
            m_ref[c, rows, :] = m_new

    def run(step):
        key_tiles_per_query_tile = tq // tk
        first_diagonal = qi * key_tiles_per_query_tile

        def body(j, carry):
            step(j, 0, False)
            return carry
        lax.fori_loop(0, first_diagonal, body, 0)
        for r in range(key_tiles_per_query_tile):
            step(first_diagonal + r, r * tk, True)

    use_fixed = shift <= MAX_FIXED_SHIFT

    @pl.when(use_fixed)
    def _():
        run(fixed_step)

    @pl.when(jnp.logical_not(use_fixed))
    def _():
        m_ref[...] = jnp.full(m_ref.shape, -jnp.inf, F32)
        l_ref[...] = jnp.zeros(l_ref.shape, F32)
        run(online_step)
        for c in range(2):
            acc_ref[c, :, V_DIM:2 * V_DIM] = jnp.broadcast_to(l_ref[c], (tq, V_DIM))

    lam = (jnp.exp(jnp.sum(lq1_ref[...] * lk1_ref[...], axis=-1, keepdims=True))
           - jnp.exp(jnp.sum(lq2_ref[...] * lk2_ref[...], axis=-1, keepdims=True))
           + lam_init)
    o = (acc_ref[0, :, 0:V_DIM] / acc_ref[0, :, V_DIM:2 * V_DIM]
         - lam * (acc_ref[1, :, 0:V_DIM] / acc_ref[1, :, V_DIM:2 * V_DIM]))
    ms = jnp.mean(o * o, axis=-1, keepdims=True)
    o = (o * lax.rsqrt(ms + EPS)) * sg_ref[...] * (1.0 - lam_init)
    o_ref[0] = o.astype(BF16)


def _attn_call(shift, q, k, v, lq1, lk1, lq2, lk2, sg, lam_init):
    bsz, seq, d = q.shape
    t = ATTN_QUERY_TILE
    vec = pl.BlockSpec((1, HEAD_DIM), lambda b, h, i: (0, 0))
    kv = pl.BlockSpec((1, seq, V_DIM), lambda b, h, i: (b, 0, h))
    qo = pl.BlockSpec((1, t, V_DIM), lambda b, h, i: (b, i, h))
    return pl.pallas_call(
        functools.partial(_attn_kernel, lam_init=lam_init),
        grid=(bsz, N_HEADS, seq // t),
        in_specs=[pl.BlockSpec(memory_space=pltpu.SMEM), qo, kv, kv, vec, vec, vec, vec,
                  pl.BlockSpec((1, V_DIM), lambda b, h, i: (0, 0))],
        out_specs=qo,
        out_shape=jax.ShapeDtypeStruct(q.shape, BF16),
        scratch_shapes=[
            pltpu.VMEM((2, t, V_DIM), BF16),
            pltpu.VMEM((seq, 2 * V_DIM), BF16),
            pltpu.VMEM((2, t, 1), F32),
            pltpu.VMEM((2, t, 1), F32),
            pltpu.VMEM((2, t, 2 * V_DIM), F32),
        ],
        compiler_params=pltpu.CompilerParams(
            dimension_semantics=("arbitrary",) * 3, vmem_limit_bytes=VMEM_LIMIT_BYTES),
        name="diff_attn",
    )(shift, q, k, v, lq1, lk1, lq2, lk2, sg)


def _merge_kernel(x_ref, mod_ref, o_ref, yc_ref, ga_ref, wao_ref, wo_ref, out_ref):
    y_attn = _dot(o_ref[0], wao_ref[...])
    merged = yc_ref[0].astype(F32) + ga_ref[0].astype(F32) * y_attn
    y = _dot(merged.astype(BF16), wo_ref[...])
    out_ref[0] = x_ref[0] + mod_ref[0, 5:6, :] * y


def _merge_call(x, mod, o, yc, ga, wao, wo):
    bsz, seq, d = x.shape
    tm = TOKEN_TILE
    tile = pl.BlockSpec((1, tm, d), lambda b, i: (b, i, 0))
    return pl.pallas_call(
        _merge_kernel,
        grid=(bsz, seq // tm),
        in_specs=[tile, pl.BlockSpec((1, N_MOD, d), lambda b, i: (b, 0, 0)),
                  tile, tile, tile, _resident(wao.shape), _resident(wo.shape)],
        out_specs=tile,
        out_shape=jax.ShapeDtypeStruct(x.shape, F32),
        compiler_params=_params(),
        name="merge",
    )(x, mod, o, yc, ga, wao, wo)


def _rope_tables(positions):
    half = HEAD_DIM // 2
    inv_freq = 1.0 / (ROPE_THETA ** (jnp.arange(0, half, dtype=F32) * (2.0 / HEAD_DIM)))
    ang = positions.astype(F32)[..., None] * inv_freq
    cos, sin = jnp.cos(ang), jnp.sin(ang)
    zero = jnp.zeros_like(sin)
    reps = V7X_LANES // HEAD_DIM
    return jnp.concatenate([cos] * (2 * reps) + [-sin, zero] * reps + [zero, sin] * reps, axis=-1)


def _group_mean_matrix():
    g = np.arange(V7X_MXU_DIM) // HEAD_DIM
    return jnp.asarray((g[:, None] == g[None, :]).astype(np.float32) / HEAD_DIM, dtype=BF16)


def kernel(x, c, positions, norm_g, w_ada, b_ada, w_ffn1_gu, w_ffn1_down, w_in, conv_w, q_norm_g, k_norm_g, lambda_q1, lambda_k1, lambda_q2, lambda_k2, subln_g, w_conv_out, w_attn_out, w_o, w_ffn2_gu, w_ffn2_down):
    bsz, seq, d = x.shape
    depth = norm_g.shape[0]
    assert seq % TOKEN_TILE == 0 and seq % ATTN_QUERY_TILE == 0 and d % COL_CHUNK == 0
    assert w_ffn1_down.shape[1] % COL_CHUNK == 0

    mods = _ada_call(c, w_ada, b_ada).reshape(depth, bsz, N_MOD, d)
    rope = _rope_tables(positions)
    gmat = _group_mean_matrix()
    reps = V7X_LANES // HEAD_DIM
    half = HEAD_DIM // 2

    def lane_rows(g, scale):
        partner = jnp.concatenate([g[half:], g[:half]])
        return [jnp.tile(g, reps) * scale, jnp.tile(partner, reps) * scale]

    for l in range(depth):
        lam_init = 0.8 - 0.6 * float(np.exp(-0.3 * l))
        mod = mods[l]
        ng = norm_g[l]
        q_gain = HEAD_DIM ** -0.5 * LOG2_E
        qkg = jnp.stack(lane_rows(q_norm_g[l], q_gain) + lane_rows(k_norm_g[l], 1.0))
        shift = (HEAD_DIM * q_gain * jnp.max(jnp.abs(q_norm_g[l])) * jnp.max(jnp.abs(k_norm_g[l]))
                 ).reshape(1).astype(F32)
        vec = lambda a: a[l].reshape(1, -1)

        x = _ffn_call(x, mod, ng[0:1], w_ffn1_gu[l].astype(BF16), w_ffn1_down[l].astype(BF16), 0)
        q, k, v, yc, ga = _proj_call(x, mod, ng[1:2], w_in[l].astype(BF16), conv_w[l],
                                     w_conv_out[l].astype(BF16), rope, qkg, gmat)
        o = _attn_call(shift, q, k, v, vec(lambda_q1), vec(lambda_k1), vec(lambda_q2), vec(lambda_k2),
                       vec(subln_g), lam_init)
        x = _merge_call(x, mod, o, yc, ga, w_attn_out[l].astype(BF16), w_o[l].astype(BF16))
        x = _ffn_call(x, mod, ng[2:3], w_ffn2_gu[l].astype(BF16), w_ffn2_down[l].astype(BF16), 6)
    return x
```

```python
import functools

import jax
import jax.numpy as jnp
import numpy as np
from jax import lax
from jax.experimental import pallas as pl
from jax.experimental.pallas import tpu as pltpu

N_HEADS = 8
HEAD_DIM = 64
V_DIM = 2 * HEAD_DIM
ROPE_THETA = 10000.0
EPS = 1e-6
N_MOD = 9

V7X_LANES = 128
V7X_SUBLANES = 8
V7X_MXU_DIM = 256
VMEM_LIMIT_BYTES = 56 * 1024 * 1024

TOKEN_TILE = 512
COL_CHUNK = V7X_MXU_DIM
ATTN_KEY_TILE = 512
ATTN_QUERY_TILE = 2 * ATTN_KEY_TILE
ATTN_DIAG_TILE = V7X_MXU_DIM
ROPE_HALF = HEAD_DIM // 2
GROUPS_PER_BLOCK = V7X_LANES // ROPE_HALF
HEADS_PER_BLOCK = GROUPS_PER_BLOCK // 2
MAX_FIXED_SHIFT = 48.0
LOG2_E = float(np.log2(np.e))

BF16 = jnp.bfloat16
F32 = jnp.float32


def _sigmoid(x):
    return 1.0 / (1.0 + jnp.exp(-x))


def _dot(a, b):
    return jnp.dot(a, b, preferred_element_type=F32)


def _resident(shape):
    return pl.BlockSpec(shape, lambda *_: (0,) * len(shape), pipeline_mode=pl.Buffered(1))


def _params():
    return pltpu.CompilerParams(
        dimension_semantics=("arbitrary", "arbitrary"), vmem_limit_bytes=VMEM_LIMIT_BYTES)


def _modulated_norm(x, g_row, scale, shift):
    ms = jnp.mean(x * x, axis=-1, keepdims=True)
    return (x * lax.rsqrt(ms + EPS)) * g_row * (1.0 + scale) + shift


def _ada_kernel(c_ref, w_ref, b_ref, o_ref):
    c = c_ref[...]
    c_act = (c * _sigmoid(c)).astype(BF16)
    o_ref[0] = _dot(c_act, w_ref[0].astype(BF16)) + b_ref[0]


def _ada_call(c, w_ada, b_ada):
    depth, d, n = w_ada.shape
    bsz = c.shape[0]
    tn = d
    return pl.pallas_call(
        _ada_kernel,
        grid=(depth, n // tn),
        in_specs=[
            pl.BlockSpec((bsz, d), lambda l, j: (0, 0)),
            pl.BlockSpec((1, d, tn), lambda l, j: (l, 0, j)),
            pl.BlockSpec((1, 1, tn), lambda l, j: (l, 0, j)),
        ],
        out_specs=pl.BlockSpec((1, bsz, tn), lambda l, j: (l, 0, j)),
        out_shape=jax.ShapeDtypeStruct((depth, bsz, n), F32),
        compiler_params=_params(),
        name="ada_mod",
    )(c, w_ada, b_ada.reshape(depth, 1, n))


def _ffn_kernel(x_ref, mod_ref, ng_ref, wgu_ref, wd_ref, o_ref, h_ref, a_ref, *, mod_base):
    d_ff = wd_ref.shape[0]
    shift = mod_ref[0, mod_base:mod_base + 1, :]
    scale = mod_ref[0, mod_base + 1:mod_base + 2, :]
    gate = mod_ref[0, mod_base + 2:mod_base + 3, :]
    h_ref[...] = _modulated_norm(x_ref[0], ng_ref[...], scale, shift).astype(BF16)
    for lo in range(0, d_ff, COL_CHUNK):
        h = h_ref[...]
        g = _dot(h, wgu_ref[:, lo:lo + COL_CHUNK])
        u = _dot(h, wgu_ref[:, d_ff + lo:d_ff + lo + COL_CHUNK])
        a_ref[:, lo:lo + COL_CHUNK] = (g * _sigmoid(g) * u).astype(BF16)
    y = _dot(a_ref[...], wd_ref[...])
    o_ref[0] = x_ref[0] + 0.5 * gate * y


def _ffn_call(x, mod, ng, wgu, wd, mod_base):
    bsz, seq, d = x.shape
    d_ff = wd.shape[0]
    tm = TOKEN_TILE
    tile = pl.BlockSpec((1, tm, d), lambda b, i: (b, i, 0))
    return pl.pallas_call(
        functools.partial(_ffn_kernel, mod_base=mod_base),
        grid=(bsz, seq // tm),
        in_specs=[
            tile,
            pl.BlockSpec((1, N_MOD, d), lambda b, i: (b, 0, 0)),
            _resident((1, d)),
            _resident((d, 2 * d_ff)),
            _resident((d_ff, d)),
        ],
        out_specs=tile,
        out_shape=jax.ShapeDtypeStruct(x.shape, F32),
        scratch_shapes=[pltpu.VMEM((tm, d), BF16), pltpu.VMEM((tm, d_ff), BF16)],
        compiler_params=_params(),
        name="ffn",
    )(x, mod, ng, wgu, wd)


def _proj_kernel(x_ref, mod_ref, ng_ref, win_ref, cw_ref, wco_ref, rope_ref, qkg_ref, gmat_ref,
                 q_ref, k_ref, v_ref, yc_ref, ga_ref, h_ref, upad_ref, ypre_ref):
    tm, d = h_ref.shape
    halo = V7X_SUBLANES

    @pl.when(pl.program_id(1) == 0)
    def _():
        upad_ref[0:halo, :] = jnp.zeros((halo, d), F32)

    shift = mod_ref[0, 3:4, :]
    scale = mod_ref[0, 4:5, :]
    h_ref[...] = _modulated_norm(x_ref[0], ng_ref[...], scale, shift).astype(BF16)

    def proj(col):
        return _dot(h_ref[...], win_ref[:, col:col + COL_CHUNK])

    for lo in range(0, d, COL_CHUNK):
        cols = slice(lo, lo + COL_CHUNK)
        b_c = proj(lo)
        u = proj(d + lo) * proj(2 * d + lo)
        upad_ref[halo:halo + tm, cols] = u
        conv = (cw_ref[2:3, cols] * u
                + cw_ref[1:2, cols] * upad_ref[halo - 1:halo - 1 + tm, cols]
                + cw_ref[0:1, cols] * upad_ref[halo - 2:halo - 2 + tm, cols])
        upad_ref[0:halo, cols] = upad_ref[tm:tm + halo, cols]
        ypre_ref[:, cols] = (b_c * conv).astype(BF16)
    for lo in range(0, d, COL_CHUNK):
        cols = slice(lo, lo + COL_CHUNK)
        y_conv = _dot(ypre_ref[...], wco_ref[:, cols])
        yc_ref[0, :, cols] = (_sigmoid(proj(6 * d + lo)) * y_conv).astype(BF16)

    cos_t = rope_ref[0, :, 0:V7X_LANES]
    sin_t = rope_ref[0, :, V7X_LANES:2 * V7X_LANES]
    for base, out_ref, row in ((3 * d, q_ref, 0), (4 * d, k_ref, 2)):
        g_first = qkg_ref[row:row + 1, :]
        g_second = qkg_ref[row + 1:row + 2, :]
        cos_a, sin_a = cos_t * g_first, sin_t * g_first
        cos_b, sin_b = cos_t * g_second, sin_t * g_second
        for lo in range(0, d, 2 * COL_CHUNK):
            blocks = [proj(base + lo), proj(base + lo + COL_CHUNK)]
            squares = [blk[:, :V7X_LANES] * blk[:, :V7X_LANES] + blk[:, V7X_LANES:] * blk[:, V7X_LANES:]
                       for blk in blocks]
            ms = _dot(jnp.concatenate(squares, axis=1).astype(BF16), gmat_ref[...])
            rinv = lax.rsqrt(ms + EPS)
            for i, blk in enumerate(blocks):
                r = rinv[:, i * V7X_LANES:(i + 1) * V7X_LANES]
                first = blk[:, :V7X_LANES] * r
                second = blk[:, V7X_LANES:] * r
                col = lo + i * COL_CHUNK
                out_ref[0, :, col:col + V7X_LANES] = (first * cos_a - second * sin_b).astype(BF16)
                out_ref[0, :, col + V7X_LANES:col + COL_CHUNK] = (
                    second * cos_b + first * sin_a).astype(BF16)

    for lo in range(0, d, COL_CHUNK):
        cols = slice(lo, lo + COL_CHUNK)
        v_ref[0, :, cols] = proj(5 * d + lo).astype(BF16)
        ga_ref[0, :, cols] = _sigmoid(proj(7 * d + lo)).astype(BF16)


def _proj_call(x, mod, ng, win, cw, wco, rope, qkg, gmat):
    bsz, seq, d = x.shape
    tm = TOKEN_TILE
    tile = pl.BlockSpec((1, tm, d), lambda b, i: (b, i, 0))
    act = jax.ShapeDtypeStruct(x.shape, BF16)
    return pl.pallas_call(
        _proj_kernel,
        grid=(bsz, seq // tm),
        in_specs=[
            tile,
            pl.BlockSpec((1, N_MOD, d), lambda b, i: (b, 0, 0)),
            _resident((1, d)),
            _resident(win.shape),
            _resident(cw.shape),
            _resident(wco.shape),
            pl.BlockSpec((1, tm, 2 * V7X_LANES), lambda b, i: (b, i, 0)),
            _resident(qkg.shape),
            _resident(gmat.shape),
        ],
        out_specs=[tile] * 5,
        out_shape=[act] * 5,
        scratch_shapes=[
            pltpu.VMEM((tm, d), BF16),
            pltpu.VMEM((tm + V7X_SUBLANES, d), F32),
            pltpu.VMEM((tm, d), BF16),
        ],
        compiler_params=_params(),
        name="proj",
    )(x, mod, ng, win, cw, wco, rope, qkg, gmat)


def _attn_kernel(shift_ref, q_ref, k_ref, v_ref, lq1_ref, lk1_ref, lq2_ref, lk2_ref, sg_ref, o_ref,
                 qc_ref, vaug_ref, m_ref, l_ref, acc_ref, *, lam_init):
    tq = q_ref.shape[1]
    qi = pl.program_id(2)
    shift = shift_ref[0]

    @pl.when(qi == 0)
    def _():
        for hh in range(HEADS_PER_BLOCK):
            vaug_ref[hh, :, 0:V_DIM] = v_ref[0, :, hh * V_DIM:(hh + 1) * V_DIM]
            vaug_ref[hh, :, V_DIM:2 * V_DIM] = jnp.ones((vaug_ref.shape[1], V_DIM), BF16)

    lane = lax.broadcasted_iota(jnp.int32, q_ref.shape[1:], 1)
    lane_group = jnp.right_shift(jnp.bitwise_and(lane, V7X_LANES - 1), 5)
    q = q_ref[0]
    zero = jnp.zeros_like(q)
    for g in range(GROUPS_PER_BLOCK):
        qc_ref[g] = jnp.where(lane_group == g, q, zero)

    def scores(g, rows, start, nk):
        return lax.dot_general(qc_ref[g, rows, :], k_ref[0, pl.ds(start, nk), :],
                               (((1,), (1,)), ((), ())), preferred_element_type=F32)

    def causal(x, fill):
        row = lax.broadcasted_iota(jnp.int32, x.shape, 0)
        col = lax.broadcasted_iota(jnp.int32, x.shape, 1)
        return jnp.where(row >= col, x, fill)

    def fixed_step(start, nk, row0, on_diagonal, first):
        rows = slice(row0, tq)
        for g in range(GROUPS_PER_BLOCK):
            p = jnp.exp2(scores(g, rows, start, nk) - shift)
            if on_diagonal:
                p = causal(p, 0.0)
            pv = _dot(p.astype(BF16), vaug_ref[g // 2, pl.ds(start, nk), :])
            acc_ref[g, rows, :] = pv if first else acc_ref[g, rows, :] + pv

    def online_step(start, nk, row0, on_diagonal, first):
        rows = slice(row0, tq)
        for g in range(GROUPS_PER_BLOCK):
            s = scores(g, rows, start, nk)
            if on_diagonal:
                s = causal(s, -jnp.inf)
            m_prev = m_ref[g, rows, :]
            m_new = jnp.maximum(m_prev, jnp.max(s, axis=1, keepdims=True))
            alpha = jnp.exp2(m_prev - m_new)
            p = jnp.exp2(s - m_new)
            l_ref[g, rows, :] = alpha * l_ref[g, rows, :] + jnp.sum(p, axis=1, keepdims=True)
            acc_ref[g, rows, 0:V_DIM] = (
                alpha * acc_ref[g, rows, 0:V_DIM]
                + _dot(p.astype(BF16), vaug_ref[g // 2, pl.ds(start, nk), 0:V_DIM]))
            m_ref[g, rows, :] = m_new

    def run(step):
        tk, td = ATTN_KEY_TILE, ATTN_DIAG_TILE
        for u in range(tq // td):
            step(pl.multiple_of(qi * tq + u * td, td), td, u * td, True, u == 0)

        def body(j, carry):
            for r in range(tq // tk):
                step(pl.multiple_of(j * tq + r * tk, tk), tk, 0, False, False)
            return carry
        lax.fori_loop(0, qi, body, 0)

    use_fixed = shift <= MAX_FIXED_SHIFT

    @pl.when(use_fixed)
    def _():
        run(fixed_step)

    @pl.when(jnp.logical_not(use_fixed))
    def _():
        m_ref[...] = jnp.full(m_ref.shape, -jnp.inf, F32)
        l_ref[...] = jnp.zeros(l_ref.shape, F32)
        acc_ref[...] = jnp.zeros(acc_ref.shape, F32)
        run(online_step)
        for g in range(GROUPS_PER_BLOCK):
            acc_ref[g, :, V_DIM:2 * V_DIM] = jnp.broadcast_to(l_ref[g], (tq, V_DIM))

    lam = (jnp.exp(jnp.sum(lq1_ref[...] * lk1_ref[...], axis=-1, keepdims=True))
           - jnp.exp(jnp.sum(lq2_ref[...] * lk2_ref[...], axis=-1, keepdims=True))
           + lam_init)
    for hh in range(HEADS_PER_BLOCK):
        g1, g2 = 2 * hh, 2 * hh + 1
        o = (acc_ref[g1, :, 0:V_DIM] / acc_ref[g1, :, V_DIM:2 * V_DIM]
             - lam * (acc_ref[g2, :, 0:V_DIM] / acc_ref[g2, :, V_DIM:2 * V_DIM]))
        ms = jnp.mean(o * o, axis=-1, keepdims=True)
        o = (o * lax.rsqrt(ms + EPS)) * sg_ref[...] * (1.0 - lam_init)
        o_ref[0, :, hh * V_DIM:(hh + 1) * V_DIM] = o.astype(BF16)


def _attn_call(shift, q, k, v, lq1, lk1, lq2, lk2, sg, lam_init):
    bsz, seq, d = q.shape
    t = ATTN_QUERY_TILE
    width = HEADS_PER_BLOCK * V_DIM
    vec = pl.BlockSpec((1, HEAD_DIM), lambda b, h, i: (0, 0))
    kv = pl.BlockSpec((1, seq, width), lambda b, h, i: (b, 0, h))
    qo = pl.BlockSpec((1, t, width), lambda b, h, i: (b, i, h))
    return pl.pallas_call(
        functools.partial(_attn_kernel, lam_init=lam_init),
        grid=(bsz, d // width, seq // t),
        in_specs=[pl.BlockSpec(memory_space=pltpu.SMEM), qo, kv, kv, vec, vec, vec, vec,
                  pl.BlockSpec((1, V_DIM), lambda b, h, i: (0, 0))],
        out_specs=qo,
        out_shape=jax.ShapeDtypeStruct(q.shape, BF16),
        scratch_shapes=[
            pltpu.VMEM((GROUPS_PER_BLOCK, t, width), BF16),
            pltpu.VMEM((HEADS_PER_BLOCK, seq, 2 * V_DIM), BF16),
            pltpu.VMEM((GROUPS_PER_BLOCK, t, 1), F32),
            pltpu.VMEM((GROUPS_PER_BLOCK, t, 1), F32),
            pltpu.VMEM((GROUPS_PER_BLOCK, t, 2 * V_DIM), F32),
        ],
        compiler_params=pltpu.CompilerParams(
            dimension_semantics=("arbitrary",) * 3, vmem_limit_bytes=VMEM_LIMIT_BYTES),
        name="diff_attn",
    )(shift, q, k, v, lq1, lk1, lq2, lk2, sg)


def _merge_kernel(x_ref, mod_ref, o_ref, yc_ref, ga_ref, wao_ref, wo_ref, out_ref):
    y_attn = _dot(o_ref[0], wao_ref[...])
    merged = yc_ref[0].astype(F32) + ga_ref[0].astype(F32) * y_attn
    y = _dot(merged.astype(BF16), wo_ref[...])
    out_ref[0] = x_ref[0] + mod_ref[0, 5:6, :] * y


def _merge_call(x, mod, o, yc, ga, wao, wo):
    bsz, seq, d = x.shape
    tm = TOKEN_TILE
    tile = pl.BlockSpec((1, tm, d), lambda b, i: (b, i, 0))
    return pl.pallas_call(
        _merge_kernel,
        grid=(bsz, seq // tm),
        in_specs=[tile, pl.BlockSpec((1, N_MOD, d), lambda b, i: (b, 0, 0)),
                  tile, tile, tile, _resident(wao.shape), _resident(wo.shape)],
        out_specs=tile,
        out_shape=jax.ShapeDtypeStruct(x.shape, F32),
        compiler_params=_params(),
        name="merge",
    )(x, mod, o, yc, ga, wao, wo)


def _rope_tables(positions):
    inv_freq = 1.0 / (ROPE_THETA ** (jnp.arange(0, ROPE_HALF, dtype=F32) * (2.0 / HEAD_DIM)))
    ang = positions.astype(F32)[..., None] * inv_freq
    return jnp.concatenate([jnp.cos(ang)] * GROUPS_PER_BLOCK + [jnp.sin(ang)] * GROUPS_PER_BLOCK,
                           axis=-1)


def _group_mean_matrix():
    g = np.arange(V7X_MXU_DIM) // ROPE_HALF
    return jnp.asarray((g[:, None] == g[None, :]).astype(np.float32) / HEAD_DIM, dtype=BF16)


def _pair_rotary_halves(w):
    rows = w.shape[0]
    w = w.reshape(rows, -1, GROUPS_PER_BLOCK, 2, ROPE_HALF)
    return w.transpose(0, 1, 3, 2, 4).reshape(rows, -1)


def kernel(x, c, positions, norm_g, w_ada, b_ada, w_ffn1_gu, w_ffn1_down, w_in, conv_w, q_norm_g, k_norm_g, lambda_q1, lambda_k1, lambda_q2, lambda_k2, subln_g, w_conv_out, w_attn_out, w_o, w_ffn2_gu, w_ffn2_down):
    bsz, seq, d = x.shape
    depth = norm_g.shape[0]
    assert seq % TOKEN_TILE == 0 and seq % ATTN_QUERY_TILE == 0 and d % COL_CHUNK == 0
    assert w_ffn1_down.shape[1] % COL_CHUNK == 0

    mods = _ada_call(c, w_ada, b_ada).reshape(depth, bsz, N_MOD, d)
    rope = _rope_tables(positions)
    gmat = _group_mean_matrix()

    def lane_rows(g, scale):
        return [jnp.tile(g[:ROPE_HALF], GROUPS_PER_BLOCK) * scale,
                jnp.tile(g[ROPE_HALF:], GROUPS_PER_BLOCK) * scale]

    for l in range(depth):
        lam_init = 0.8 - 0.6 * float(np.exp(-0.3 * l))
        mod = mods[l]
        ng = norm_g[l]
        q_gain = HEAD_DIM ** -0.5 * LOG2_E
        qkg = jnp.stack(lane_rows(q_norm_g[l], q_gain) + lane_rows(k_norm_g[l], 1.0))
        shift = (HEAD_DIM * q_gain * jnp.max(jnp.abs(q_norm_g[l])) * jnp.max(jnp.abs(k_norm_g[l]))
                 ).reshape(1).astype(F32)
        vec = lambda a: a[l].reshape(1, -1)

        x = _ffn_call(x, mod, ng[0:1], w_ffn1_gu[l].astype(BF16), w_ffn1_down[l].astype(BF16), 0)
        win = jnp.concatenate([w_in[l, :, :3 * d], _pair_rotary_halves(w_in[l, :, 3 * d:4 * d]),
                               _pair_rotary_halves(w_in[l, :, 4 * d:5 * d]), w_in[l, :, 5 * d:]],
                              axis=1).astype(BF16)
        q, k, v, yc, ga = _proj_call(x, mod, ng[1:2], win, conv_w[l],
                                     w_conv_out[l].astype(BF16), rope, qkg, gmat)
        o = _attn_call(shift, q, k, v, vec(lambda_q1), vec(lambda_k1), vec(lambda_q2), vec(lambda_k2),
                       vec(subln_g), lam_init)
        x = _merge_call(x, mod, o, yc, ga, w_attn_out[l].astype(BF16), w_o[l].astype(BF16))
        x = _ffn_call(x, mod, ng[2:3], w_ffn2_gu[l].astype(BF16), w_ffn2_down[l].astype(BF16), 6)
    return x
```

```python
import functools

import jax
import jax.numpy as jnp
import numpy as np
from jax import lax
from jax.experimental import pallas as pl
from jax.experimental.pallas import tpu as pltpu

N_HEADS = 8
HEAD_DIM = 64
V_DIM = 2 * HEAD_DIM
ROPE_THETA = 10000.0
EPS = 1e-6
N_MOD = 9

V7X_LANES = 128
V7X_SUBLANES = 8
V7X_MXU_DIM = 256
VMEM_LIMIT_BYTES = 56 * 1024 * 1024

TOKEN_TILE = 512
MERGE_TILE = 1024
FFN_TILE = 1024
COL_CHUNK = V7X_MXU_DIM
ATTN_KEY_TILE = 512
ATTN_QUERY_TILE = 2 * ATTN_KEY_TILE
ATTN_DIAG_TILE = V7X_MXU_DIM
ROPE_HALF = HEAD_DIM // 2
GROUPS_PER_BLOCK = V7X_LANES // ROPE_HALF
HEADS_PER_BLOCK = GROUPS_PER_BLOCK // 2
MAX_FIXED_SHIFT = 48.0
LOG2_E = float(np.log2(np.e))

BF16 = jnp.bfloat16
F32 = jnp.float32


def _sigmoid(x):
    return 1.0 / (1.0 + jnp.exp(-x))


def _dot(a, b):
    return jnp.dot(a, b, preferred_element_type=F32)


def _resident(shape):
    return pl.BlockSpec(shape, lambda *_: (0,) * len(shape), pipeline_mode=pl.Buffered(1))


def _resident_layer(stacked, layer):
    shape = stacked.shape[1:]
    return pl.BlockSpec((None,) + shape, lambda *_: (layer,) + (0,) * len(shape),
                        pipeline_mode=pl.Buffered(1))


def _mod_spec(mods, layer):
    return pl.BlockSpec((None, 1) + mods.shape[2:], lambda b, i: (layer, b, 0, 0))


def _params():
    return pltpu.CompilerParams(
        dimension_semantics=("arbitrary", "arbitrary"), vmem_limit_bytes=VMEM_LIMIT_BYTES)


def _modulated_norm(x, g_row, scale, shift):
    ms = jnp.mean(x * x, axis=-1, keepdims=True)
    return (x * lax.rsqrt(ms + EPS)) * g_row * (1.0 + scale) + shift


def _ada_kernel(c_ref, w_ref, b_ref, o_ref):
    c = c_ref[...]
    c_act = (c * _sigmoid(c)).astype(BF16)
    o_ref[0] = _dot(c_act, w_ref[0].astype(BF16)) + b_ref[0]


def _ada_call(c, w_ada, b_ada):
    depth, d, n = w_ada.shape
    bsz = c.shape[0]
    tn = d
    return pl.pallas_call(
        _ada_kernel,
        grid=(depth, n // tn),
        in_specs=[
            pl.BlockSpec((bsz, d), lambda l, j: (0, 0)),
            pl.BlockSpec((1, d, tn), lambda l, j: (l, 0, j)),
            pl.BlockSpec((1, 1, tn), lambda l, j: (l, 0, j)),
        ],
        out_specs=pl.BlockSpec((1, bsz, tn), lambda l, j: (l, 0, j)),
        out_shape=jax.ShapeDtypeStruct((depth, bsz, n), F32),
        compiler_params=_params(),
        name="ada_mod",
    )(c, w_ada, b_ada.reshape(depth, 1, n))


def _ffn_kernel(x_ref, mod_ref, ng_ref, wgu_ref, wd_ref, o_ref, h_ref, a_ref, *, mod_base):
    d_ff = wd_ref.shape[0]
    shift = mod_ref[0, mod_base:mod_base + 1, :]
    scale = mod_ref[0, mod_base + 1:mod_base + 2, :]
    gate = mod_ref[0, mod_base + 2:mod_base + 3, :]
    h_ref[...] = _modulated_norm(x_ref[0], ng_ref[...], scale, shift).astype(BF16)
    for lo in range(0, d_ff, COL_CHUNK):
        h = h_ref[...]
        g = _dot(h, wgu_ref[:, lo:lo + COL_CHUNK])
        u = _dot(h, wgu_ref[:, d_ff + lo:d_ff + lo + COL_CHUNK])
        a_ref[:, lo:lo + COL_CHUNK] = (g * _sigmoid(g) * u).astype(BF16)
    y = _dot(a_ref[...], wd_ref[...])
    o_ref[0] = x_ref[0] + 0.5 * gate * y


def _ffn_call(x, mods, ng, wgu, wd, layer, mod_base):
    bsz, seq, d = x.shape
    d_ff = wd.shape[1]
    tm = FFN_TILE
    tile = pl.BlockSpec((1, tm, d), lambda b, i: (b, i, 0))
    return pl.pallas_call(
        functools.partial(_ffn_kernel, mod_base=mod_base),
        grid=(bsz, seq // tm),
        in_specs=[
            tile,
            _mod_spec(mods, layer),
            _resident((1, d)),
            _resident_layer(wgu, layer),
            _resident_layer(wd, layer),
        ],
        out_specs=tile,
        out_shape=jax.ShapeDtypeStruct(x.shape, F32),
        scratch_shapes=[pltpu.VMEM((tm, d), BF16), pltpu.VMEM((tm, d_ff), BF16)],
        compiler_params=_params(),
        name="ffn",
    )(x, mods, ng, wgu, wd)


def _proj_kernel(x_ref, mod_ref, ng_ref, win_ref, cw_ref, wco_ref, rope_ref, qkg_ref, gmat_ref,
                 q_ref, k_ref, v_ref, yc_ref, ga_ref, h_ref, upad_ref, ypre_ref):
    tm, d = h_ref.shape
    halo = V7X_SUBLANES

    @pl.when(pl.program_id(1) == 0)
    def _():
        upad_ref[0:halo, :] = jnp.zeros((halo, d), F32)

    shift = mod_ref[0, 3:4, :]
    scale = mod_ref[0, 4:5, :]
    h_ref[...] = _modulated_norm(x_ref[0], ng_ref[...], scale, shift).astype(BF16)

    def proj(col):
        return _dot(h_ref[...], win_ref[:, col:col + COL_CHUNK])

    for lo in range(0, d, COL_CHUNK):
        cols = slice(lo, lo + COL_CHUNK)
        b_c = proj(lo)
        u = proj(d + lo) * proj(2 * d + lo)
        upad_ref[halo:halo + tm, cols] = u
        conv = (cw_ref[2:3, cols] * u
                + cw_ref[1:2, cols] * upad_ref[halo - 1:halo - 1 + tm, cols]
                + cw_ref[0:1, cols] * upad_ref[halo - 2:halo - 2 + tm, cols])
        upad_ref[0:halo, cols] = upad_ref[tm:tm + halo, cols]
        ypre_ref[:, cols] = (b_c * conv).astype(BF16)
    for lo in range(0, d, COL_CHUNK):
        cols = slice(lo, lo + COL_CHUNK)
        y_conv = _dot(ypre_ref[...], wco_ref[:, cols])
        yc_ref[0, :, cols] = (_sigmoid(proj(6 * d + lo)) * y_conv).astype(BF16)

    cos_t = rope_ref[0, :, 0:V7X_LANES]
    sin_t = rope_ref[0, :, V7X_LANES:2 * V7X_LANES]
    for base, out_ref, row in ((3 * d, q_ref, 0), (4 * d, k_ref, 2)):
        g_first = qkg_ref[row:row + 1, :]
        g_second = qkg_ref[row + 1:row + 2, :]
        cos_a, sin_a = cos_t * g_first, sin_t * g_first
        cos_b, sin_b = cos_t * g_second, sin_t * g_second
        for lo in range(0, d, 2 * COL_CHUNK):
            blocks = [proj(base + lo), proj(base + lo + COL_CHUNK)]
            squares = [blk[:, :V7X_LANES] * blk[:, :V7X_LANES] + blk[:, V7X_LANES:] * blk[:, V7X_LANES:]
                       for blk in blocks]
            ms = _dot(jnp.concatenate(squares, axis=1).astype(BF16), gmat_ref[...])
            rinv = lax.rsqrt(ms + EPS)
            for i, blk in enumerate(blocks):
                r = rinv[:, i * V7X_LANES:(i + 1) * V7X_LANES]
                first = blk[:, :V7X_LANES] * r
                second = blk[:, V7X_LANES:] * r
                col = lo + i * COL_CHUNK
                out_ref[0, :, col:col + V7X_LANES] = (first * cos_a - second * sin_b).astype(BF16)
                out_ref[0, :, col + V7X_LANES:col + COL_CHUNK] = (
                    second * cos_b + first * sin_a).astype(BF16)

    for lo in range(0, d, COL_CHUNK):
        cols = slice(lo, lo + COL_CHUNK)
        v_ref[0, :, cols] = proj(5 * d + lo).astype(BF16)
        ga_ref[0, :, cols] = _sigmoid(proj(7 * d + lo)).astype(BF16)


def _proj_call(x, mods, ng, win, cw, wco, rope, qkg, gmat, layer):
    bsz, seq, d = x.shape
    tm = TOKEN_TILE
    tile = pl.BlockSpec((1, tm, d), lambda b, i: (b, i, 0))
    act = jax.ShapeDtypeStruct(x.shape, BF16)
    return pl.pallas_call(
        _proj_kernel,
        grid=(bsz, seq // tm),
        in_specs=[
            tile,
            _mod_spec(mods, layer),
            _resident((1, d)),
            _resident_layer(win, layer),
            _resident(cw.shape),
            _resident_layer(wco, layer),
            pl.BlockSpec((1, tm, 2 * V7X_LANES), lambda b, i: (b, i, 0)),
            _resident(qkg.shape),
            _resident(gmat.shape),
        ],
        out_specs=[tile] * 5,
        out_shape=[act] * 5,
        scratch_shapes=[
            pltpu.VMEM((tm, d), BF16),
            pltpu.VMEM((tm + V7X_SUBLANES, d), F32),
            pltpu.VMEM((tm, d), BF16),
        ],
        compiler_params=_params(),
        name="proj",
    )(x, mods, ng, win, cw, wco, rope, qkg, gmat)


def _attn_kernel(shift_ref, q_ref, k_ref, v_ref, lq1_ref, lk1_ref, lq2_ref, lk2_ref, sg_ref, mean_ref,
                 o_ref, qc_ref, vaug_ref, m_ref, l_ref, acc_ref, *, lam_init):
    tq = q_ref.shape[1]
    qi = pl.program_id(2)
    shift = shift_ref[0]

    @pl.when(qi == 0)
    def _():
        for hh in range(HEADS_PER_BLOCK):
            vaug_ref[hh, :, 0:V_DIM] = v_ref[0, :, hh * V_DIM:(hh + 1) * V_DIM]
            vaug_ref[hh, :, V_DIM:2 * V_DIM] = jnp.ones((vaug_ref.shape[1], V_DIM), BF16)

    lane = lax.broadcasted_iota(jnp.int32, q_ref.shape[1:], 1)
    lane_group = jnp.right_shift(jnp.bitwise_and(lane, V7X_LANES - 1), 5)
    q = q_ref[0]
    zero = jnp.zeros_like(q)
    for g in range(GROUPS_PER_BLOCK):
        qc_ref[g] = jnp.where(lane_group == g, q, zero)

    def scores(g, rows, start, nk):
        return lax.dot_general(qc_ref[g, rows, :], k_ref[0, pl.ds(start, nk), :],
                               (((1,), (1,)), ((), ())), preferred_element_type=F32)

    def causal(x, fill):
        row = lax.broadcasted_iota(jnp.int32, x.shape, 0)
        col = lax.broadcasted_iota(jnp.int32, x.shape, 1)
        return jnp.where(row >= col, x, fill)

    def fixed_step(start, nk, row0, on_diagonal, first):
        rows = slice(row0, tq)
        for g in range(GROUPS_PER_BLOCK):
            p = jnp.exp2(scores(g, rows, start, nk) - shift)
            if on_diagonal:
                p = causal(p, 0.0)
            pv = _dot(p.astype(BF16), vaug_ref[g // 2, pl.ds(start, nk), :])
            acc_ref[g, rows, :] = pv if first else acc_ref[g, rows, :] + pv

    def online_step(start, nk, row0, on_diagonal, first):
        rows = slice(row0, tq)
        for g in range(GROUPS_PER_BLOCK):
            s = scores(g, rows, start, nk)
            if on_diagonal:
                s = causal(s, -jnp.inf)
            m_prev = m_ref[g, rows, :]
            m_new = jnp.maximum(m_prev, jnp.max(s, axis=1, keepdims=True))
            alpha = jnp.exp2(m_prev - m_new)
            p = jnp.exp2(s - m_new)
            l_ref[g, rows, :] = alpha * l_ref[g, rows, :] + jnp.sum(p, axis=1, keepdims=True)
            acc_ref[g, rows, 0:V_DIM] = (
                alpha * acc_ref[g, rows, 0:V_DIM]
                + _dot(p.astype(BF16), vaug_ref[g // 2, pl.ds(start, nk), 0:V_DIM]))
            m_ref[g, rows, :] = m_new

    def run(step):
        tk, td = ATTN_KEY_TILE, ATTN_DIAG_TILE
        for u in range(tq // td):
            step(pl.multiple_of(qi * tq + u * td, td), td, u * td, True, u == 0)

        def body(j, carry):
            for r in range(tq // tk):
                step(pl.multiple_of(j * tq + r * tk, tk), tk, 0, False, False)
            return carry
        lax.fori_loop(0, qi, body, 0)

    use_fixed = shift <= MAX_FIXED_SHIFT

    @pl.when(use_fixed)
    def _():
        run(fixed_step)

    @pl.when(jnp.logical_not(use_fixed))
    def _():
        m_ref[...] = jnp.full(m_ref.shape, -jnp.inf, F32)
        l_ref[...] = jnp.zeros(l_ref.shape, F32)
        acc_ref[...] = jnp.zeros(acc_ref.shape, F32)
        run(online_step)
        for g in range(GROUPS_PER_BLOCK):
            acc_ref[g, :, V_DIM:2 * V_DIM] = jnp.broadcast_to(l_ref[g], (tq, V_DIM))

    lam = (jnp.exp(jnp.sum(lq1_ref[...] * lk1_ref[...], axis=-1, keepdims=True))
           - jnp.exp(jnp.sum(lq2_ref[...] * lk2_ref[...], axis=-1, keepdims=True))
           + lam_init)
    for hh in range(HEADS_PER_BLOCK):
        g1, g2 = 2 * hh, 2 * hh + 1
        o = (acc_ref[g1, :, 0:V_DIM] / acc_ref[g1, :, V_DIM:2 * V_DIM]
             - lam * (acc_ref[g2, :, 0:V_DIM] / acc_ref[g2, :, V_DIM:2 * V_DIM]))
        ms = _dot((o * o).astype(BF16), mean_ref[...])
        o = (o * lax.rsqrt(ms + EPS)) * sg_ref[...] * (1.0 - lam_init)
        o_ref[0, :, hh * V_DIM:(hh + 1) * V_DIM] = o.astype(BF16)


def _attn_call(shift, q, k, v, lq1, lk1, lq2, lk2, sg, lam_init):
    bsz, seq, d = q.shape
    t = ATTN_QUERY_TILE
    width = HEADS_PER_BLOCK * V_DIM
    vec = pl.BlockSpec((1, HEAD_DIM), lambda b, h, i: (0, 0))
    kv = pl.BlockSpec((1, seq, width), lambda b, h, i: (b, 0, h))
    qo = pl.BlockSpec((1, t, width), lambda b, h, i: (b, i, h))
    return pl.pallas_call(
        functools.partial(_attn_kernel, lam_init=lam_init),
        grid=(bsz, d // width, seq // t),
        in_specs=[pl.BlockSpec(memory_space=pltpu.SMEM), qo, kv, kv, vec, vec, vec, vec,
                  pl.BlockSpec((1, V_DIM), lambda b, h, i: (0, 0)),
                  pl.BlockSpec((V_DIM, V_DIM), lambda b, h, i: (0, 0))],
        out_specs=qo,
        out_shape=jax.ShapeDtypeStruct(q.shape, BF16),
        scratch_shapes=[
            pltpu.VMEM((GROUPS_PER_BLOCK, t, width), BF16),
            pltpu.VMEM((HEADS_PER_BLOCK, seq, 2 * V_DIM), BF16),
            pltpu.VMEM((GROUPS_PER_BLOCK, t, 1), F32),
            pltpu.VMEM((GROUPS_PER_BLOCK, t, 1), F32),
            pltpu.VMEM((GROUPS_PER_BLOCK, t, 2 * V_DIM), F32),
        ],
        compiler_params=pltpu.CompilerParams(
            dimension_semantics=("arbitrary",) * 3, vmem_limit_bytes=VMEM_LIMIT_BYTES),
        name="diff_attn",
    )(shift, q, k, v, lq1, lk1, lq2, lk2, sg, jnp.full((V_DIM, V_DIM), 1.0 / V_DIM, BF16))


def _merge_kernel(x_ref, mod_ref, o_ref, yc_ref, ga_ref, wao_ref, wo_ref, out_ref):
    y_attn = _dot(o_ref[0], wao_ref[...])
    merged = yc_ref[0].astype(F32) + ga_ref[0].astype(F32) * y_attn
    y = _dot(merged.astype(BF16), wo_ref[...])
    out_ref[0] = x_ref[0] + mod_ref[0, 5:6, :] * y


def _merge_call(x, mods, o, yc, ga, wao, wo, layer):
    bsz, seq, d = x.shape
    tm = MERGE_TILE
    tile = pl.BlockSpec((1, tm, d), lambda b, i: (b, i, 0))
    return pl.pallas_call(
        _merge_kernel,
        grid=(bsz, seq // tm),
        in_specs=[tile, _mod_spec(mods, layer), tile, tile, tile,
                  _resident_layer(wao, layer), _resident_layer(wo, layer)],
        out_specs=tile,
        out_shape=jax.ShapeDtypeStruct(x.shape, F32),
        compiler_params=_params(),
        name="merge",
    )(x, mods, o, yc, ga, wao, wo)


def _rope_tables(positions):
    inv_freq = 1.0 / (ROPE_THETA ** (jnp.arange(0, ROPE_HALF, dtype=F32) * (2.0 / HEAD_DIM)))
    ang = positions.astype(F32)[..., None] * jnp.tile(inv_freq, GROUPS_PER_BLOCK)
    return jnp.concatenate([jnp.cos(ang), jnp.sin(ang)], axis=-1)


def _group_mean_matrix():
    g = np.arange(V7X_MXU_DIM) // ROPE_HALF
    return jnp.asarray((g[:, None] == g[None, :]).astype(np.float32) / HEAD_DIM, dtype=BF16)


def _pair_rotary_halves(w):
    lead = w.shape[:-1]
    w = w.reshape(*lead, -1, GROUPS_PER_BLOCK, 2, ROPE_HALF)
    return jnp.swapaxes(w, -3, -2).reshape(*lead, -1)


def kernel(x, c, positions, norm_g, w_ada, b_ada, w_ffn1_gu, w_ffn1_down, w_in, conv_w, q_norm_g, k_norm_g, lambda_q1, lambda_k1, lambda_q2, lambda_k2, subln_g, w_conv_out, w_attn_out, w_o, w_ffn2_gu, w_ffn2_down):
    bsz, seq, d = x.shape
    depth = norm_g.shape[0]
    assert seq % TOKEN_TILE == 0 and seq % FFN_TILE == 0 and seq % MERGE_TILE == 0 and seq % ATTN_QUERY_TILE == 0 and d % COL_CHUNK == 0
    assert w_ffn1_down.shape[1] % COL_CHUNK == 0 and w_ffn2_down.shape[1] % COL_CHUNK == 0

    mods = _ada_call(c, w_ada, b_ada).reshape(depth, bsz, N_MOD, d)
    rope = _rope_tables(positions)
    gmat = _group_mean_matrix()

    def lane_rows(g, scale):
        return [jnp.tile(g[:ROPE_HALF], GROUPS_PER_BLOCK) * scale,
                jnp.tile(g[ROPE_HALF:], GROUPS_PER_BLOCK) * scale]

    wgu1, wd1 = w_ffn1_gu.astype(BF16), w_ffn1_down.astype(BF16)
    wgu2, wd2 = w_ffn2_gu.astype(BF16), w_ffn2_down.astype(BF16)
    wco, wao, wo = w_conv_out.astype(BF16), w_attn_out.astype(BF16), w_o.astype(BF16)
    win = jnp.concatenate([w_in[..., :3 * d], _pair_rotary_halves(w_in[..., 3 * d:4 * d]),
                           _pair_rotary_halves(w_in[..., 4 * d:5 * d]), w_in[..., 5 * d:]],
                          axis=-1).astype(BF16)

    for l in range(depth):
        lam_init = 0.8 - 0.6 * float(np.exp(-0.3 * l))
        ng = norm_g[l]
        q_gain = HEAD_DIM ** -0.5 * LOG2_E
        qkg = jnp.stack(lane_rows(q_norm_g[l], q_gain) + lane_rows(k_norm_g[l], 1.0))
        shift = (HEAD_DIM * q_gain * jnp.max(jnp.abs(q_norm_g[l])) * jnp.max(jnp.abs(k_norm_g[l]))
                 ).reshape(1).astype(F32)
        vec = lambda a: a[l].reshape(1, -1)

        x = _ffn_call(x, mods, ng[0:1], wgu1, wd1, l, 0)
        q, k, v, yc, ga = _proj_call(x, mods, ng[1:2], win, conv_w[l], wco, rope, qkg, gmat, l)
        o = _attn_call(shift, q, k, v, vec(lambda_q1), vec(lambda_k1), vec(lambda_q2), vec(lambda_k2),
                       vec(subln_g), lam_init)
        x = _merge_call(x, mods, o, yc, ga, wao, wo, l)
        x = _ffn_call(x, mods, ng[2:3], wgu2, wd2, l, 6)
    return x
```

```python
import functools

import jax
import jax.numpy as jnp
import numpy as np
from jax import lax
from jax.experimental import pallas as pl
from jax.experimental.pallas import tpu as pltpu

N_HEADS = 8
HEAD_DIM = 64
V_DIM = 2 * HEAD_DIM
ROPE_THETA = 10000.0
EPS = 1e-6
N_MOD = 9

V7X_LANES = 128
V7X_SUBLANES = 8
V7X_MXU_DIM = 256
VMEM_LIMIT_BYTES = 56 * 1024 * 1024

TOKEN_TILE = 512
MERGE_TILE = 1024
FFN_TILE = 1024
WARMUP_ROWS = 256
BF16_SUBLANES = 16
COL_CHUNK = V7X_MXU_DIM
ATTN_KEY_TILE = 512
ATTN_QUERY_TILE = 2 * ATTN_KEY_TILE
ATTN_DIAG_TILE = V7X_MXU_DIM
ROPE_HALF = HEAD_DIM // 2
GROUPS_PER_BLOCK = V7X_LANES // ROPE_HALF
HEADS_PER_BLOCK = GROUPS_PER_BLOCK // 2
MAX_FIXED_SHIFT = 48.0
LOG2_E = float(np.log2(np.e))

BF16 = jnp.bfloat16
F32 = jnp.float32


def _sigmoid(x):
    return 1.0 / (1.0 + jnp.exp(-x))


def _dot(a, b):
    return jnp.dot(a, b, preferred_element_type=F32)


def _resident(shape):
    return pl.BlockSpec(shape, lambda *_: (0,) * len(shape), pipeline_mode=pl.Buffered(1))


def _resident_layer(stacked, layer):
    shape = stacked.shape[1:]
    return pl.BlockSpec((None,) + shape, lambda *_: (layer,) + (0,) * len(shape),
                        pipeline_mode=pl.Buffered(1))


def _mod_spec(mods, layer):
    return pl.BlockSpec((None, 1) + mods.shape[2:], lambda b, i: (layer, b, 0, 0))


def _cast_specs(stack, layer, grid):
    _, rows, cols = stack.shape
    n_steps = int(np.prod(grid))
    units = rows // BF16_SUBLANES
    per_chunk = next(k for k in range(1, units + 1) if units % k == 0 and units // k <= n_steps)
    chunk_rows = BF16_SUBLANES * per_chunk
    n_chunks = rows // chunk_rows
    steps_per_chunk = n_steps // n_chunks

    def chunk(b, h, i):
        step = (b * grid[1] + h) * grid[2] + i
        return jnp.minimum(step // steps_per_chunk, n_chunks - 1)

    return (pl.BlockSpec((None, chunk_rows, cols), lambda b, h, i: (layer, chunk(b, h, i), 0)),
            pl.BlockSpec((chunk_rows, cols), lambda b, h, i: (chunk(b, h, i), 0)),
            jax.ShapeDtypeStruct((rows, cols), BF16))


def _params():
    return pltpu.CompilerParams(
        dimension_semantics=("arbitrary", "arbitrary"), vmem_limit_bytes=VMEM_LIMIT_BYTES)


def _modulated_norm(x, g_row, scale, shift):
    ms = jnp.mean(x * x, axis=-1, keepdims=True)
    return (x * lax.rsqrt(ms + EPS)) * g_row * (1.0 + scale) + shift


def _warm_up_rows(tm):
    return [slice(r, r + WARMUP_ROWS) for r in range(0, tm, WARMUP_ROWS)]


def _ada_kernel(c_ref, w_ref, b_ref, o_ref):
    c = c_ref[...]
    c_act = (c * _sigmoid(c)).astype(BF16)
    o_ref[0] = _dot(c_act, w_ref[0].astype(BF16)) + b_ref[0]


def _ada_call(c, w_ada, b_ada):
    depth, d, n = w_ada.shape
    bsz = c.shape[0]
    tn = d
    return pl.pallas_call(
        _ada_kernel,
        grid=(depth, n // tn),
        in_specs=[
            pl.BlockSpec((bsz, d), lambda l, j: (0, 0)),
            pl.BlockSpec((1, d, tn), lambda l, j: (l, 0, j)),
            pl.BlockSpec((1, 1, tn), lambda l, j: (l, 0, j)),
        ],
        out_specs=pl.BlockSpec((1, bsz, tn), lambda l, j: (l, 0, j)),
        out_shape=jax.ShapeDtypeStruct((depth, bsz, n), F32),
        compiler_params=_params(),
        name="ada_mod",
    )(c, w_ada, b_ada.reshape(depth, 1, n))


def _ffn_kernel(x_ref, mod_ref, ng_ref, wgu_ref, wd_ref, o_ref, h_ref, a_ref, *, mod_base):
    d_ff = wd_ref.shape[0]
    shift = mod_ref[0, mod_base:mod_base + 1, :]
    scale = mod_ref[0, mod_base + 1:mod_base + 2, :]
    gate = mod_ref[0, mod_base + 2:mod_base + 3, :]

    def hidden(rows, lo):
        h = h_ref[rows, :]
        g = _dot(h, wgu_ref[:, lo:lo + COL_CHUNK])
        u = _dot(h, wgu_ref[:, d_ff + lo:d_ff + lo + COL_CHUNK])
        a_ref[rows, lo:lo + COL_CHUNK] = (g * _sigmoid(g) * u).astype(BF16)

    for rows in _warm_up_rows(h_ref.shape[0]):
        h_ref[rows, :] = _modulated_norm(x_ref[0, rows, :], ng_ref[...], scale, shift).astype(BF16)
        hidden(rows, 0)
    for lo in range(COL_CHUNK, d_ff, COL_CHUNK):
        hidden(slice(None), lo)
    y = _dot(a_ref[...], wd_ref[...])
    o_ref[0] = x_ref[0] + 0.5 * gate * y


def _ffn_call(x, mods, ng, wgu, wd, layer, mod_base):
    bsz, seq, d = x.shape
    d_ff = wd.shape[0]
    tm = FFN_TILE
    tile = pl.BlockSpec((1, tm, d), lambda b, i: (b, i, 0))
    return pl.pallas_call(
        functools.partial(_ffn_kernel, mod_base=mod_base),
        grid=(bsz, seq // tm),
        in_specs=[
            tile,
            _mod_spec(mods, layer),
            _resident((1, d)),
            _resident(wgu.shape),
            _resident(wd.shape),
        ],
        out_specs=tile,
        out_shape=jax.ShapeDtypeStruct(x.shape, F32),
        scratch_shapes=[pltpu.VMEM((tm, d), BF16), pltpu.VMEM((tm, d_ff), BF16)],
        compiler_params=_params(),
        name="ffn",
    )(x, mods, ng, wgu, wd)


def _proj_kernel(x_ref, mod_ref, ng_ref, win_ref, wqk_ref, cw_ref, wco_ref, rope_ref, qkg_ref, gmat_ref,
                 q_ref, k_ref, v_ref, yc_ref, ga_ref, h_ref, upad_ref, ypre_ref):
    tm, d = h_ref.shape
    halo = V7X_SUBLANES

    @pl.when(pl.program_id(1) == 0)
    def _():
        upad_ref[0:halo, :] = jnp.zeros((halo, d), F32)

    shift = mod_ref[0, 3:4, :]
    scale = mod_ref[0, 4:5, :]

    def proj(col, rows=slice(None)):
        return _dot(h_ref[rows, :], win_ref[:, col:col + COL_CHUNK])

    def value_and_gate(rows, lo):
        cols = slice(lo, lo + COL_CHUNK)
        v_ref[0, rows, cols] = proj(5 * d + lo, rows).astype(BF16)
        ga_ref[0, rows, cols] = _sigmoid(proj(7 * d + lo, rows)).astype(BF16)

    for rows in _warm_up_rows(tm):
        h_ref[rows, :] = _modulated_norm(x_ref[0, rows, :], ng_ref[...], scale, shift).astype(BF16)
        value_and_gate(rows, 0)
    for lo in range(COL_CHUNK, d, COL_CHUNK):
        value_and_gate(slice(None), lo)

    for lo in range(0, d, COL_CHUNK):
        cols = slice(lo, lo + COL_CHUNK)
        b_c = proj(lo)
        u = proj(d + lo) * proj(2 * d + lo)
        upad_ref[halo:halo + tm, cols] = u
        conv = (cw_ref[2:3, cols] * u
                + cw_ref[1:2, cols] * upad_ref[halo - 1:halo - 1 + tm, cols]
                + cw_ref[0:1, cols] * upad_ref[halo - 2:halo - 2 + tm, cols])
        upad_ref[0:halo, cols] = upad_ref[tm:tm + halo, cols]
        ypre_ref[:, cols] = (b_c * conv).astype(BF16)
    for lo in range(0, d, COL_CHUNK):
        cols = slice(lo, lo + COL_CHUNK)
        y_conv = _dot(ypre_ref[...], wco_ref[:, cols])
        yc_ref[0, :, cols] = (_sigmoid(proj(6 * d + lo)) * y_conv).astype(BF16)

    cos_t = rope_ref[0, :, 0:V7X_LANES]
    sin_t = rope_ref[0, :, V7X_LANES:2 * V7X_LANES]
    def proj_qk(col):
        return _dot(h_ref[...], wqk_ref[:, col:col + COL_CHUNK])

    for base, out_ref, row in ((0, q_ref, 0), (d, k_ref, 2)):
        g_first = qkg_ref[row:row + 1, :]
        g_second = qkg_ref[row + 1:row + 2, :]
        cos_a, sin_a = cos_t * g_first, sin_t * g_first
        cos_b, sin_b = cos_t * g_second, sin_t * g_second
        for lo in range(0, d, 2 * COL_CHUNK):
            blocks = [proj_qk(base + lo), proj_qk(base + lo + COL_CHUNK)]
            squares = [blk[:, :V7X_LANES] * blk[:, :V7X_LANES] + blk[:, V7X_LANES:] * blk[:, V7X_LANES:]
                       for blk in blocks]
            ms = _dot(jnp.concatenate(squares, axis=1).astype(BF16), gmat_ref[...])
            rinv = lax.rsqrt(ms + EPS)
            for i, blk in enumerate(blocks):
                r = rinv[:, i * V7X_LANES:(i + 1) * V7X_LANES]
                first = blk[:, :V7X_LANES] * r
                second = blk[:, V7X_LANES:] * r
                col = lo + i * COL_CHUNK
                out_ref[0, :, col:col + V7X_LANES] = (first * cos_a - second * sin_b).astype(BF16)
                out_ref[0, :, col + V7X_LANES:col + COL_CHUNK] = (
                    second * cos_b + first * sin_a).astype(BF16)


def _proj_call(x, mods, ng, win, wqk, cw, wco, rope, qkg, gmat, layer):
    bsz, seq, d = x.shape
    tm = TOKEN_TILE
    tile = pl.BlockSpec((1, tm, d), lambda b, i: (b, i, 0))
    act = jax.ShapeDtypeStruct(x.shape, BF16)
    return pl.pallas_call(
        _proj_kernel,
        grid=(bsz, seq // tm),
        in_specs=[
            tile,
            _mod_spec(mods, layer),
            _resident((1, d)),
            _resident(win.shape),
            _resident_layer(wqk, layer),
            _resident(cw.shape),
            _resident(wco.shape),
            pl.BlockSpec((1, tm, 2 * V7X_LANES), lambda b, i: (b, i, 0)),
            _resident(qkg.shape),
            _resident(gmat.shape),
        ],
        out_specs=[tile] * 5,
        out_shape=[act] * 5,
        scratch_shapes=[
            pltpu.VMEM((tm, d), BF16),
            pltpu.VMEM((tm + V7X_SUBLANES, d), F32),
            pltpu.VMEM((tm, d), BF16),
        ],
        compiler_params=_params(),
        name="proj",
    )(x, mods, ng, win, wqk, cw, wco, rope, qkg, gmat)


def _attn_kernel(shift_ref, q_ref, k_ref, v_ref, lq1_ref, lk1_ref, lq2_ref, lk2_ref, sg_ref, mean_ref,
                 *rest, lam_init, n_cast):
    cast_src, rest = rest[:n_cast], rest[n_cast:]
    o_ref, rest = rest[0], rest[1:]
    cast_dst, (qc_ref, vaug_ref, m_ref, l_ref, acc_ref) = rest[:n_cast], rest[n_cast:]
    tq = q_ref.shape[1]
    qi = pl.program_id(2)
    shift = shift_ref[0]

    @pl.when(qi == 0)
    def _():
        for hh in range(HEADS_PER_BLOCK):
            vaug_ref[hh, :, 0:V_DIM] = v_ref[0, :, hh * V_DIM:(hh + 1) * V_DIM]
            vaug_ref[hh, :, V_DIM:2 * V_DIM] = jnp.ones((vaug_ref.shape[1], V_DIM), BF16)

    lane = lax.broadcasted_iota(jnp.int32, q_ref.shape[1:], 1)
    lane_group = jnp.right_shift(jnp.bitwise_and(lane, V7X_LANES - 1), 5)
    q = q_ref[0]
    zero = jnp.zeros_like(q)
    for g in range(GROUPS_PER_BLOCK):
        qc_ref[g] = jnp.where(lane_group == g, q, zero)

    def scores(g, rows, start, nk):
        return lax.dot_general(qc_ref[g, rows, :], k_ref[0, pl.ds(start, nk), :],
                               (((1,), (1,)), ((), ())), preferred_element_type=F32)

    def causal(x, fill):
        row = lax.broadcasted_iota(jnp.int32, x.shape, 0)
        col = lax.broadcasted_iota(jnp.int32, x.shape, 1)
        return jnp.where(row >= col, x, fill)

    def fixed_step(start, nk, row0, on_diagonal, first):
        rows = slice(row0, tq)
        for g in range(GROUPS_PER_BLOCK):
            p = jnp.exp2(scores(g, rows, start, nk) - shift)
            if on_diagonal:
                p = causal(p, 0.0)
            pv = _dot(p.astype(BF16), vaug_ref[g // 2, pl.ds(start, nk), :])
            acc_ref[g, rows, :] = pv if first else acc_ref[g, rows, :] + pv

    def online_step(start, nk, row0, on_diagonal, first):
        rows = slice(row0, tq)
        for g in range(GROUPS_PER_BLOCK):
            s = scores(g, rows, start, nk)
            if on_diagonal:
                s = causal(s, -jnp.inf)
            m_prev = m_ref[g, rows, :]
            m_new = jnp.maximum(m_prev, jnp.max(s, axis=1, keepdims=True))
            alpha = jnp.exp2(m_prev - m_new)
            p = jnp.exp2(s - m_new)
            l_ref[g, rows, :] = alpha * l_ref[g, rows, :] + jnp.sum(p, axis=1, keepdims=True)
            acc_ref[g, rows, 0:V_DIM] = (
                alpha * acc_ref[g, rows, 0:V_DIM]
                + _dot(p.astype(BF16), vaug_ref[g // 2, pl.ds(start, nk), 0:V_DIM]))
            m_ref[g, rows, :] = m_new

    def run(step):
        tk, td = ATTN_KEY_TILE, ATTN_DIAG_TILE
        for u in range(tq // td):
            step(pl.multiple_of(qi * tq + u * td, td), td, u * td, True, u == 0)
        for src_ref, dst_ref in zip(cast_src, cast_dst):
            dst_ref[...] = src_ref[...].astype(BF16)

        def body(j, carry):
            for r in range(tq // tk):
                step(pl.multiple_of(j * tq + r * tk, tk), tk, 0, False, False)
            return carry
        lax.fori_loop(0, qi, body, 0)

    use_fixed = shift <= MAX_FIXED_SHIFT

    @pl.when(use_fixed)
    def _():
        run(fixed_step)

    @pl.when(jnp.logical_not(use_fixed))
    def _():
        m_ref[...] = jnp.full(m_ref.shape, -jnp.inf, F32)
        l_ref[...] = jnp.zeros(l_ref.shape, F32)
        acc_ref[...] = jnp.zeros(acc_ref.shape, F32)
        run(online_step)
        for g in range(GROUPS_PER_BLOCK):
            acc_ref[g, :, V_DIM:2 * V_DIM] = jnp.broadcast_to(l_ref[g], (tq, V_DIM))

    lam = (jnp.exp(jnp.sum(lq1_ref[...] * lk1_ref[...], axis=-1, keepdims=True))
           - jnp.exp(jnp.sum(lq2_ref[...] * lk2_ref[...], axis=-1, keepdims=True))
           + lam_init)
    for hh in range(HEADS_PER_BLOCK):
        g1, g2 = 2 * hh, 2 * hh + 1
        o = (acc_ref[g1, :, 0:V_DIM] / acc_ref[g1, :, V_DIM:2 * V_DIM]
             - lam * (acc_ref[g2, :, 0:V_DIM] / acc_ref[g2, :, V_DIM:2 * V_DIM]))
        ms = _dot((o * o).astype(BF16), mean_ref[...])
        o = (o * lax.rsqrt(ms + EPS)) * sg_ref[...] * (1.0 - lam_init)
        o_ref[0, :, hh * V_DIM:(hh + 1) * V_DIM] = o.astype(BF16)


def _attn_call(shift, q, k, v, lq1, lk1, lq2, lk2, sg, lam_init, cast_stacks, cast_layer):
    bsz, seq, d = q.shape
    t = ATTN_QUERY_TILE
    width = HEADS_PER_BLOCK * V_DIM
    grid = (bsz, d // width, seq // t)
    cast = [_cast_specs(s, cast_layer, grid) for s in cast_stacks]
    vec = pl.BlockSpec((1, HEAD_DIM), lambda b, h, i: (0, 0))
    kv = pl.BlockSpec((1, seq, width), lambda b, h, i: (b, 0, h))
    qo = pl.BlockSpec((1, t, width), lambda b, h, i: (b, i, h))
    return pl.pallas_call(
        functools.partial(_attn_kernel, lam_init=lam_init, n_cast=len(cast)),
        grid=grid,
        in_specs=[pl.BlockSpec(memory_space=pltpu.SMEM), qo, kv, kv, vec, vec, vec, vec,
                  pl.BlockSpec((1, V_DIM), lambda b, h, i: (0, 0)),
                  pl.BlockSpec((V_DIM, V_DIM), lambda b, h, i: (0, 0))] + [c[0] for c in cast],
        out_specs=[qo] + [c[1] for c in cast],
        out_shape=[jax.ShapeDtypeStruct(q.shape, BF16)] + [c[2] for c in cast],
        scratch_shapes=[
            pltpu.VMEM((GROUPS_PER_BLOCK, t, width), BF16),
            pltpu.VMEM((HEADS_PER_BLOCK, seq, 2 * V_DIM), BF16),
            pltpu.VMEM((GROUPS_PER_BLOCK, t, 1), F32),
            pltpu.VMEM((GROUPS_PER_BLOCK, t, 1), F32),
            pltpu.VMEM((GROUPS_PER_BLOCK, t, 2 * V_DIM), F32),
        ],
        compiler_params=pltpu.CompilerParams(
            dimension_semantics=("arbitrary",) * 3, vmem_limit_bytes=VMEM_LIMIT_BYTES),
        name="diff_attn",
    )(shift, q, k, v, lq1, lk1, lq2, lk2, sg, jnp.full((V_DIM, V_DIM), 1.0 / V_DIM, BF16), *cast_stacks)


def _merge_kernel(x_ref, mod_ref, o_ref, yc_ref, ga_ref, wao_ref, wo_ref, out_ref):
    y_attn = _dot(o_ref[0], wao_ref[...])
    merged = yc_ref[0].astype(F32) + ga_ref[0].astype(F32) * y_attn
    y = _dot(merged.astype(BF16), wo_ref[...])
    out_ref[0] = x_ref[0] + mod_ref[0, 5:6, :] * y


def _merge_call(x, mods, o, yc, ga, wao, wo, layer):
    bsz, seq, d = x.shape
    tm = MERGE_TILE
    tile = pl.BlockSpec((1, tm, d), lambda b, i: (b, i, 0))
    return pl.pallas_call(
        _merge_kernel,
        grid=(bsz, seq // tm),
        in_specs=[tile, _mod_spec(mods, layer), tile, tile, tile,
                  _resident(wao.shape), _resident(wo.shape)],
        out_specs=tile,
        out_shape=jax.ShapeDtypeStruct(x.shape, F32),
        compiler_params=_params(),
        name="merge",
    )(x, mods, o, yc, ga, wao, wo)


def _rope_tables(positions):
    inv_freq = 1.0 / (ROPE_THETA ** (jnp.arange(0, ROPE_HALF, dtype=F32) * (2.0 / HEAD_DIM)))
    ang = positions.astype(F32)[..., None] * jnp.tile(inv_freq, GROUPS_PER_BLOCK)
    return jnp.concatenate([jnp.cos(ang), jnp.sin(ang)], axis=-1)


def _group_mean_matrix():
    g = np.arange(V7X_MXU_DIM) // ROPE_HALF
    return jnp.asarray((g[:, None] == g[None, :]).astype(np.float32) / HEAD_DIM, dtype=BF16)


def _pair_rotary_halves(w):
    lead = w.shape[:-1]
    w = w.reshape(*lead, -1, GROUPS_PER_BLOCK, 2, ROPE_HALF)
    return jnp.swapaxes(w, -3, -2).reshape(*lead, -1)


def kernel(x, c, positions, norm_g, w_ada, b_ada, w_ffn1_gu, w_ffn1_down, w_in, conv_w, q_norm_g, k_norm_g, lambda_q1, lambda_k1, lambda_q2, lambda_k2, subln_g, w_conv_out, w_attn_out, w_o, w_ffn2_gu, w_ffn2_down):
    bsz, seq, d = x.shape
    depth = norm_g.shape[0]
    assert seq % TOKEN_TILE == 0 and seq % FFN_TILE == 0 and seq % MERGE_TILE == 0 and seq % ATTN_QUERY_TILE == 0 and d % COL_CHUNK == 0
    assert w_ffn1_down.shape[1] % COL_CHUNK == 0 and w_ffn2_down.shape[1] % COL_CHUNK == 0

    mods = _ada_call(c, w_ada, b_ada).reshape(depth, bsz, N_MOD, d)
    rope = _rope_tables(positions)
    gmat = _group_mean_matrix()

    def lane_rows(g, scale):
        return [jnp.tile(g[:ROPE_HALF], GROUPS_PER_BLOCK) * scale,
                jnp.tile(g[ROPE_HALF:], GROUPS_PER_BLOCK) * scale]

    stacks = [w_ffn1_gu, w_ffn1_down, w_in, w_conv_out, w_attn_out, w_o, w_ffn2_gu, w_ffn2_down]
    weights = [s[0].astype(BF16) for s in stacks]
    wqk = jnp.concatenate([_pair_rotary_halves(w_in[..., 3 * d:4 * d].astype(BF16)),
                           _pair_rotary_halves(w_in[..., 4 * d:5 * d].astype(BF16))], axis=-1)

    for l in range(depth):
        lam_init = 0.8 - 0.6 * float(np.exp(-0.3 * l))
        ng = norm_g[l]
        q_gain = HEAD_DIM ** -0.5 * LOG2_E
        qkg = jnp.stack(lane_rows(q_norm_g[l], q_gain) + lane_rows(k_norm_g[l], 1.0))
        shift = (HEAD_DIM * q_gain * jnp.max(jnp.abs(q_norm_g[l])) * jnp.max(jnp.abs(k_norm_g[l]))
                 ).reshape(1).astype(F32)
        vec = lambda a: a[l].reshape(1, -1)

        wgu1, wd1, win, wco, wao, wo, wgu2, wd2 = weights

        x = _ffn_call(x, mods, ng[0:1], wgu1, wd1, l, 0)
        q, k, v, yc, ga = _proj_call(x, mods, ng[1:2], win, wqk, conv_w[l], wco, rope, qkg, gmat, l)
        o, *next_weights = _attn_call(
            shift, q, k, v, vec(lambda_q1), vec(lambda_k1), vec(lambda_q2), vec(lambda_k2),
            vec(subln_g), lam_init, stacks if l + 1 < depth else [], l + 1)
        x = _merge_call(x, mods, o, yc, ga, wao, wo, l)
        x = _ffn_call(x, mods, ng[2:3], wgu2, wd2, l, 6)
        weights = next_weights
    return x
```

```python
import functools

import jax
import jax.numpy as jnp
import numpy as np
from jax import lax
from jax.experimental import pallas as pl
from jax.experimental.pallas import tpu as pltpu

N_HEADS = 8
HEAD_DIM = 64
V_DIM = 2 * HEAD_DIM
ROPE_THETA = 10000.0
EPS = 1e-6
N_MOD = 9

V7X_LANES = 128
V7X_SUBLANES = 8
V7X_MXU_DIM = 256
VMEM_LIMIT_BYTES = 56 * 1024 * 1024

TOKEN_TILE = 512
MERGE_TILE = 1024
FFN_TILE = 1024
WARMUP_ROWS = 256
BF16_SUBLANES = 16
COL_CHUNK = V7X_MXU_DIM
ATTN_KEY_TILE = 512
ATTN_QUERY_TILE = 2 * ATTN_KEY_TILE
ATTN_DIAG_TILE = V7X_MXU_DIM
ROPE_HALF = HEAD_DIM // 2
GROUPS_PER_BLOCK = V7X_LANES // ROPE_HALF
HEADS_PER_BLOCK = GROUPS_PER_BLOCK // 2
MAX_FIXED_SHIFT = 48.0
LOG2_E = float(np.log2(np.e))

BF16 = jnp.bfloat16
F32 = jnp.float32


def _sigmoid(x):
    return 1.0 / (1.0 + jnp.exp(-x))


def _dot(a, b):
    return jnp.dot(a, b, preferred_element_type=F32)


def _resident(shape):
    return pl.BlockSpec(shape, lambda *_: (0,) * len(shape), pipeline_mode=pl.Buffered(1))


def _resident_layer(stacked, layer):
    shape = stacked.shape[1:]
    return pl.BlockSpec((None,) + shape, lambda *_: (layer,) + (0,) * len(shape),
                        pipeline_mode=pl.Buffered(1))


def _mod_spec(mods, layer):
    return pl.BlockSpec((None, 1) + mods.shape[2:], lambda b, i: (layer, b, 0, 0))


def _cast_specs(stack, layer, grid):
    _, rows, cols = stack.shape
    n_steps = int(np.prod(grid))
    units = rows // BF16_SUBLANES
    per_chunk = next(k for k in range(1, units + 1) if units % k == 0 and units // k <= n_steps)
    chunk_rows = BF16_SUBLANES * per_chunk
    n_chunks = rows // chunk_rows
    steps_per_chunk = n_steps // n_chunks

    def chunk(*idx):
        step = idx[0]
        for extent, i in zip(grid[1:], idx[1:]):
            step = step * extent + i
        return jnp.minimum(step // steps_per_chunk, n_chunks - 1)

    return (pl.BlockSpec((None, chunk_rows, cols), lambda *idx: (layer, chunk(*idx), 0)),
            pl.BlockSpec((chunk_rows, cols), lambda *idx: (chunk(*idx), 0)),
            jax.ShapeDtypeStruct((rows, cols), BF16))


def _params():
    return pltpu.CompilerParams(
        dimension_semantics=("arbitrary", "arbitrary"), vmem_limit_bytes=VMEM_LIMIT_BYTES)


def _modulated_norm(x, g_row, scale, shift):
    ms = jnp.mean(x * x, axis=-1, keepdims=True)
    return (x * lax.rsqrt(ms + EPS)) * g_row * (1.0 + scale) + shift


def _warm_up_rows(tm):
    return [slice(r, r + WARMUP_ROWS) for r in range(0, tm, WARMUP_ROWS)]


def _ada_kernel(c_ref, w_ref, b_ref, o_ref):
    c = c_ref[...]
    c_act = (c * _sigmoid(c)).astype(BF16)
    o_ref[0] = _dot(c_act, w_ref[0].astype(BF16)) + b_ref[0]


def _ada_call(c, w_ada, b_ada):
    depth, d, n = w_ada.shape
    bsz = c.shape[0]
    tn = d
    return pl.pallas_call(
        _ada_kernel,
        grid=(depth, n // tn),
        in_specs=[
            pl.BlockSpec((bsz, d), lambda l, j: (0, 0)),
            pl.BlockSpec((1, d, tn), lambda l, j: (l, 0, j)),
            pl.BlockSpec((1, 1, tn), lambda l, j: (l, 0, j)),
        ],
        out_specs=pl.BlockSpec((1, bsz, tn), lambda l, j: (l, 0, j)),
        out_shape=jax.ShapeDtypeStruct((depth, bsz, n), F32),
        compiler_params=_params(),
        name="ada_mod",
    )(c, w_ada, b_ada.reshape(depth, 1, n))


def _ffn_kernel(x_ref, mod_ref, ng_ref, wgu_ref, wd_ref, *rest, mod_base, n_cast):
    cast_src, rest = rest[:n_cast], rest[n_cast:]
    o_ref, rest = rest[0], rest[1:]
    cast_dst, (h_ref, a_ref) = rest[:n_cast], rest[n_cast:]
    d_ff = wd_ref.shape[0]
    shift = mod_ref[0, mod_base:mod_base + 1, :]
    scale = mod_ref[0, mod_base + 1:mod_base + 2, :]
    gate = mod_ref[0, mod_base + 2:mod_base + 3, :]

    def hidden(rows, lo):
        h = h_ref[rows, :]
        g = _dot(h, wgu_ref[:, lo:lo + COL_CHUNK])
        u = _dot(h, wgu_ref[:, d_ff + lo:d_ff + lo + COL_CHUNK])
        a_ref[rows, lo:lo + COL_CHUNK] = (g * _sigmoid(g) * u).astype(BF16)

    for rows in _warm_up_rows(h_ref.shape[0]):
        h_ref[rows, :] = _modulated_norm(x_ref[0, rows, :], ng_ref[...], scale, shift).astype(BF16)
        hidden(rows, 0)
    for lo in range(COL_CHUNK, d_ff, COL_CHUNK):
        hidden(slice(None), lo)
    for src_ref, dst_ref in zip(cast_src, cast_dst):
        dst_ref[...] = src_ref[...].astype(BF16)
    y = _dot(a_ref[...], wd_ref[...])
    o_ref[0] = x_ref[0] + 0.5 * gate * y


def _ffn_call(x, mods, ng, wgu, wd, layer, mod_base, cast_stacks=(), cast_layer=0):
    bsz, seq, d = x.shape
    d_ff = wd.shape[0]
    tm = FFN_TILE
    tile = pl.BlockSpec((1, tm, d), lambda b, i: (b, i, 0))
    grid = (bsz, seq // tm)
    cast = [_cast_specs(s, cast_layer, grid) for s in cast_stacks]
    return pl.pallas_call(
        functools.partial(_ffn_kernel, mod_base=mod_base, n_cast=len(cast)),
        grid=grid,
        in_specs=[
            tile,
            _mod_spec(mods, layer),
            _resident((1, d)),
            _resident(wgu.shape),
            _resident(wd.shape),
        ] + [c[0] for c in cast],
        out_specs=[tile] + [c[1] for c in cast],
        out_shape=[jax.ShapeDtypeStruct(x.shape, F32)] + [c[2] for c in cast],
        scratch_shapes=[pltpu.VMEM((tm, d), BF16), pltpu.VMEM((tm, d_ff), BF16)],
        compiler_params=_params(),
        name="ffn",
    )(x, mods, ng, wgu, wd, *cast_stacks)


def _proj_kernel(x_ref, mod_ref, ng_ref, win_ref, wqk_ref, cw_ref, wco_ref, rope_ref, qkg_ref, gmat_ref,
                 q_ref, k_ref, v_ref, yc_ref, ga_ref, h_ref, upad_ref, ypre_ref):
    tm, d = h_ref.shape
    halo = V7X_SUBLANES

    @pl.when(pl.program_id(1) == 0)
    def _():
        upad_ref[0:halo, :] = jnp.zeros((halo, d), F32)

    shift = mod_ref[0, 3:4, :]
    scale = mod_ref[0, 4:5, :]

    def proj(col, rows=slice(None)):
        return _dot(h_ref[rows, :], win_ref[:, col:col + COL_CHUNK])

    def value_and_gate(rows, lo):
        cols = slice(lo, lo + COL_CHUNK)
        v_ref[0, rows, cols] = proj(5 * d + lo, rows).astype(BF16)
        ga_ref[0, rows, cols] = _sigmoid(proj(7 * d + lo, rows)).astype(BF16)

    for rows in _warm_up_rows(tm):
        h_ref[rows, :] = _modulated_norm(x_ref[0, rows, :], ng_ref[...], scale, shift).astype(BF16)
        value_and_gate(rows, 0)
    for lo in range(COL_CHUNK, d, COL_CHUNK):
        value_and_gate(slice(None), lo)

    for lo in range(0, d, COL_CHUNK):
        cols = slice(lo, lo + COL_CHUNK)
        b_c = proj(lo)
        u = proj(d + lo) * proj(2 * d + lo)
        upad_ref[halo:halo + tm, cols] = u
        conv = (cw_ref[2:3, cols] * u
                + cw_ref[1:2, cols] * upad_ref[halo - 1:halo - 1 + tm, cols]
                + cw_ref[0:1, cols] * upad_ref[halo - 2:halo - 2 + tm, cols])
        upad_ref[0:halo, cols] = upad_ref[tm:tm + halo, cols]
        ypre_ref[:, cols] = (b_c * conv).astype(BF16)
    for lo in range(0, d, COL_CHUNK):
        cols = slice(lo, lo + COL_CHUNK)
        y_conv = _dot(ypre_ref[...], wco_ref[:, cols])
        yc_ref[0, :, cols] = (_sigmoid(proj(6 * d + lo)) * y_conv).astype(BF16)

    cos_t = rope_ref[0, :, 0:V7X_LANES]
    sin_t = rope_ref[0, :, V7X_LANES:2 * V7X_LANES]
    def proj_qk(col):
        return _dot(h_ref[...], wqk_ref[:, col:col + COL_CHUNK])

    for base, out_ref, row in ((0, q_ref, 0), (d, k_ref, 2)):
        g_first = qkg_ref[row:row + 1, :]
        g_second = qkg_ref[row + 1:row + 2, :]
        cos_a, sin_a = cos_t * g_first, sin_t * g_first
        cos_b, sin_b = cos_t * g_second, sin_t * g_second
        for lo in range(0, d, 2 * COL_CHUNK):
            blocks = [proj_qk(base + lo), proj_qk(base + lo + COL_CHUNK)]
            squares = [blk[:, :V7X_LANES] * blk[:, :V7X_LANES] + blk[:, V7X_LANES:] * blk[:, V7X_LANES:]
                       for blk in blocks]
            ms = _dot(jnp.concatenate(squares, axis=1).astype(BF16), gmat_ref[...])
            rinv = lax.rsqrt(ms + EPS)
            for i, blk in enumerate(blocks):
                r = rinv[:, i * V7X_LANES:(i + 1) * V7X_LANES]
                first = blk[:, :V7X_LANES] * r
                second = blk[:, V7X_LANES:] * r
                col = lo + i * COL_CHUNK
                out_ref[0, :, col:col + V7X_LANES] = (first * cos_a - second * sin_b).astype(BF16)
                out_ref[0, :, col + V7X_LANES:col + COL_CHUNK] = (
                    second * cos_b + first * sin_a).astype(BF16)


def _proj_call(x, mods, ng, win, wqk, cw, wco, rope, qkg, gmat, layer):
    bsz, seq, d = x.shape
    tm = TOKEN_TILE
    tile = pl.BlockSpec((1, tm, d), lambda b, i: (b, i, 0))
    act = jax.ShapeDtypeStruct(x.shape, BF16)
    return pl.pallas_call(
        _proj_kernel,
        grid=(bsz, seq // tm),
        in_specs=[
            tile,
            _mod_spec(mods, layer),
            _resident((1, d)),
            _resident(win.shape),
            _resident_layer(wqk, layer),
            _resident(cw.shape),
            _resident(wco.shape),
            pl.BlockSpec((1, tm, 2 * V7X_LANES), lambda b, i: (b, i, 0)),
            _resident(qkg.shape),
            _resident(gmat.shape),
        ],
        out_specs=[tile] * 5,
        out_shape=[act] * 5,
        scratch_shapes=[
            pltpu.VMEM((tm, d), BF16),
            pltpu.VMEM((tm + V7X_SUBLANES, d), F32),
            pltpu.VMEM((tm, d), BF16),
        ],
        compiler_params=_params(),
        name="proj",
    )(x, mods, ng, win, wqk, cw, wco, rope, qkg, gmat)


def _attn_kernel(shift_ref, q_ref, k_ref, v_ref, lq1_ref, lk1_ref, lq2_ref, lk2_ref, sg_ref, mean_ref,
                 o_ref, vaug_ref, m_ref, l_ref, acc_ref, *, lam_init):
    tq = q_ref.shape[1]
    qi = pl.program_id(2)
    shift = shift_ref[0]

    @pl.when(qi == 0)
    def _():
        for hh in range(HEADS_PER_BLOCK):
            vaug_ref[hh, :, 0:V_DIM] = v_ref[0, :, hh * V_DIM:(hh + 1) * V_DIM]
            vaug_ref[hh, :, V_DIM:2 * V_DIM] = jnp.ones((vaug_ref.shape[1], V_DIM), BF16)

    lane = lax.broadcasted_iota(jnp.int32, (V7X_SUBLANES, q_ref.shape[2]), 1)
    lane_group = jnp.right_shift(jnp.bitwise_and(lane, V7X_LANES - 1), 5)[0:1, :]

    def scores(g, rows, start, nk):
        q = q_ref[0, rows, :]
        q_g = jnp.where(lane_group == g, q, jnp.zeros_like(q))
        return lax.dot_general(q_g, k_ref[0, pl.ds(start, nk), :],
                               (((1,), (1,)), ((), ())), preferred_element_type=F32)

    def causal(x, fill):
        row = lax.broadcasted_iota(jnp.int32, x.shape, 0)
        col = lax.broadcasted_iota(jnp.int32, x.shape, 1)
        return jnp.where(row >= col, x, fill)

    def fixed_step(start, nk, row0, on_diagonal, first):
        rows = slice(row0, tq)
        for g in range(GROUPS_PER_BLOCK):
            p = jnp.exp2(scores(g, rows, start, nk) - shift)
            if on_diagonal:
                p = causal(p, 0.0)
            pv = _dot(p.astype(BF16), vaug_ref[g // 2, pl.ds(start, nk), :])
            acc_ref[g, rows, :] = pv if first else acc_ref[g, rows, :] + pv

    def online_step(start, nk, row0, on_diagonal, first):
        rows = slice(row0, tq)
        for g in range(GROUPS_PER_BLOCK):
            s = scores(g, rows, start, nk)
            if on_diagonal:
                s = causal(s, -jnp.inf)
            m_prev = m_ref[g, rows, :]
            m_new = jnp.maximum(m_prev, jnp.max(s, axis=1, keepdims=True))
            alpha = jnp.exp2(m_prev - m_new)
            p = jnp.exp2(s - m_new)
            l_ref[g, rows, :] = alpha * l_ref[g, rows, :] + jnp.sum(p, axis=1, keepdims=True)
            acc_ref[g, rows, 0:V_DIM] = (
                alpha * acc_ref[g, rows, 0:V_DIM]
                + _dot(p.astype(BF16), vaug_ref[g // 2, pl.ds(start, nk), 0:V_DIM]))
            m_ref[g, rows, :] = m_new

    def run(step):
        tk, td = ATTN_KEY_TILE, ATTN_DIAG_TILE
        for u in range(tq // td):
            step(pl.multiple_of(qi * tq + u * td, td), td, u * td, True, u == 0)

        def body(j, carry):
            for r in range(tq // tk):
                step(pl.multiple_of(j * tq + r * tk, tk), tk, 0, False, False)
            return carry
        lax.fori_loop(0, qi, body, 0)

    use_fixed = shift <= MAX_FIXED_SHIFT

    @pl.when(use_fixed)
    def _():
        run(fixed_step)

    @pl.when(jnp.logical_not(use_fixed))
    def _():
        m_ref[...] = jnp.full(m_ref.shape, -jnp.inf, F32)
        l_ref[...] = jnp.zeros(l_ref.shape, F32)
        acc_ref[...] = jnp.zeros(acc_ref.shape, F32)
        run(online_step)
        for g in range(GROUPS_PER_BLOCK):
            acc_ref[g, :, V_DIM:2 * V_DIM] = jnp.broadcast_to(l_ref[g], (tq, V_DIM))

    lam = (jnp.exp(jnp.sum(lq1_ref[...] * lk1_ref[...], axis=-1, keepdims=True))
           - jnp.exp(jnp.sum(lq2_ref[...] * lk2_ref[...], axis=-1, keepdims=True))
           + lam_init)
    for hh in range(HEADS_PER_BLOCK):
        g1, g2 = 2 * hh, 2 * hh + 1
        o = (acc_ref[g1, :, 0:V_DIM] / acc_ref[g1, :, V_DIM:2 * V_DIM]
             - lam * (acc_ref[g2, :, 0:V_DIM] / acc_ref[g2, :, V_DIM:2 * V_DIM]))
        ms = _dot((o * o).astype(BF16), mean_ref[...])
        o = (o * lax.rsqrt(ms + EPS)) * sg_ref[...] * (1.0 - lam_init)
        o_ref[0, :, hh * V_DIM:(hh + 1) * V_DIM] = o.astype(BF16)


def _attn_call(shift, q, k, v, lq1, lk1, lq2, lk2, sg, lam_init):
    bsz, seq, d = q.shape
    t = ATTN_QUERY_TILE
    width = HEADS_PER_BLOCK * V_DIM
    vec = pl.BlockSpec((1, HEAD_DIM), lambda b, h, i: (0, 0))
    kv = pl.BlockSpec((1, seq, width), lambda b, h, i: (b, 0, h))
    qo = pl.BlockSpec((1, t, width), lambda b, h, i: (b, i, h))
    return pl.pallas_call(
        functools.partial(_attn_kernel, lam_init=lam_init),
        grid=(bsz, d // width, seq // t),
        in_specs=[pl.BlockSpec(memory_space=pltpu.SMEM), qo, kv, kv, vec, vec, vec, vec,
                  pl.BlockSpec((1, V_DIM), lambda b, h, i: (0, 0)),
                  pl.BlockSpec((V_DIM, V_DIM), lambda b, h, i: (0, 0))],
        out_specs=qo,
        out_shape=jax.ShapeDtypeStruct(q.shape, BF16),
        scratch_shapes=[
            pltpu.VMEM((HEADS_PER_BLOCK, seq, 2 * V_DIM), BF16),
            pltpu.VMEM((GROUPS_PER_BLOCK, t, 1), F32),
            pltpu.VMEM((GROUPS_PER_BLOCK, t, 1), F32),
            pltpu.VMEM((GROUPS_PER_BLOCK, t, 2 * V_DIM), F32),
        ],
        compiler_params=pltpu.CompilerParams(
            dimension_semantics=("arbitrary",) * 3, vmem_limit_bytes=VMEM_LIMIT_BYTES),
        name="diff_attn",
    )(shift, q, k, v, lq1, lk1, lq2, lk2, sg, jnp.full((V_DIM, V_DIM), 1.0 / V_DIM, BF16))


def _merge_kernel(x_ref, mod_ref, o_ref, yc_ref, ga_ref, wao_ref, wo_ref, out_ref):
    y_attn = _dot(o_ref[0], wao_ref[...])
    merged = yc_ref[0].astype(F32) + ga_ref[0].astype(F32) * y_attn
    y = _dot(merged.astype(BF16), wo_ref[...])
    out_ref[0] = x_ref[0] + mod_ref[0, 5:6, :] * y


def _merge_call(x, mods, o, yc, ga, wao, wo, layer):
    bsz, seq, d = x.shape
    tm = MERGE_TILE
    tile = pl.BlockSpec((1, tm, d), lambda b, i: (b, i, 0))
    return pl.pallas_call(
        _merge_kernel,
        grid=(bsz, seq // tm),
        in_specs=[tile, _mod_spec(mods, layer), tile, tile, tile,
                  _resident(wao.shape), _resident(wo.shape)],
        out_specs=tile,
        out_shape=jax.ShapeDtypeStruct(x.shape, F32),
        compiler_params=_params(),
        name="merge",
    )(x, mods, o, yc, ga, wao, wo)


def _rope_tables(positions):
    inv_freq = 1.0 / (ROPE_THETA ** (jnp.arange(0, ROPE_HALF, dtype=F32) * (2.0 / HEAD_DIM)))
    ang = positions.astype(F32)[..., None] * jnp.tile(inv_freq, GROUPS_PER_BLOCK)
    return jnp.concatenate([jnp.cos(ang), jnp.sin(ang)], axis=-1)


def _group_mean_matrix():
    g = np.arange(V7X_MXU_DIM) // ROPE_HALF
    return jnp.asarray((g[:, None] == g[None, :]).astype(np.float32) / HEAD_DIM, dtype=BF16)


def _pair_rotary_halves(w):
    lead = w.shape[:-1]
    w = w.reshape(*lead, -1, GROUPS_PER_BLOCK, 2, ROPE_HALF)
    return jnp.swapaxes(w, -3, -2).reshape(*lead, -1)


def kernel(x, c, positions, norm_g, w_ada, b_ada, w_ffn1_gu, w_ffn1_down, w_in, conv_w, q_norm_g, k_norm_g, lambda_q1, lambda_k1, lambda_q2, lambda_k2, subln_g, w_conv_out, w_attn_out, w_o, w_ffn2_gu, w_ffn2_down):
    bsz, seq, d = x.shape
    depth = norm_g.shape[0]
    assert seq % TOKEN_TILE == 0 and seq % FFN_TILE == 0 and seq % MERGE_TILE == 0 and seq % ATTN_QUERY_TILE == 0 and d % COL_CHUNK == 0
    assert w_ffn1_down.shape[1] % COL_CHUNK == 0 and w_ffn2_down.shape[1] % COL_CHUNK == 0

    mods = _ada_call(c, w_ada, b_ada).reshape(depth, bsz, N_MOD, d)
    rope = _rope_tables(positions)
    gmat = _group_mean_matrix()

    def lane_rows(g, scale):
        return [jnp.tile(g[:ROPE_HALF], GROUPS_PER_BLOCK) * scale,
                jnp.tile(g[ROPE_HALF:], GROUPS_PER_BLOCK) * scale]

    stacks = [w_ffn1_gu, w_ffn1_down, w_in, w_conv_out, w_attn_out, w_o, w_ffn2_gu, w_ffn2_down]
    weights = [s[0].astype(BF16) for s in stacks]
    wqk = jnp.concatenate([_pair_rotary_halves(w_in[..., 3 * d:4 * d].astype(BF16)),
                           _pair_rotary_halves(w_in[..., 4 * d:5 * d].astype(BF16))], axis=-1)

    for l in range(depth):
        lam_init = 0.8 - 0.6 * float(np.exp(-0.3 * l))
        ng = norm_g[l]
        q_gain = HEAD_DIM ** -0.5 * LOG2_E
        qkg = jnp.stack(lane_rows(q_norm_g[l], q_gain) + lane_rows(k_norm_g[l], 1.0))
        shift = (HEAD_DIM * q_gain * jnp.max(jnp.abs(q_norm_g[l])) * jnp.max(jnp.abs(k_norm_g[l]))
                 ).reshape(1).astype(F32)
        vec = lambda a: a[l].reshape(1, -1)

        wgu1, wd1, win, wco, wao, wo, wgu2, wd2 = weights

        x, *next_weights = _ffn_call(x, mods, ng[0:1], wgu1, wd1, l, 0,
                                     stacks if l + 1 < depth else (), l + 1)
        q, k, v, yc, ga = _proj_call(x, mods, ng[1:2], win, wqk, conv_w[l], wco, rope, qkg, gmat, l)
        o = _attn_call(shift, q, k, v, vec(lambda_q1), vec(lambda_k1), vec(lambda_q2), vec(lambda_k2),
                       vec(subln_g), lam_init)
        x = _merge_call(x, mods, o, yc, ga, wao, wo, l)
        x, = _ffn_call(x, mods, ng[2:3], wgu2, wd2, l, 6)
        weights = next_weights
    return x
```
